```python
import math
import jax, jax.numpy as jnp
from jax import lax
import numpy as np

D_MODEL = 4096
BATCH = 4
SEQ = 4096
DEPTH = 1

MEM_LEN = 256
DIFF_HEADS = 8
DIFF_WIDTH = D_MODEL // 2
DIFF_HEAD_DIM = DIFF_WIDTH // (2 * DIFF_HEADS)
CONV_CH = D_MODEL - DIFF_WIDTH
CONV_WIDTH = 31
MIX_WIDTH = DIFF_WIDTH + CONV_CH
IN_COLS = 3 * DIFF_WIDTH + 2 * CONV_CH
Q_BLOCK = 128
X_HEADS = 4
X_HEAD_DIM = D_MODEL // X_HEADS
N_EXPERTS = 32
TOP_K = 4
D_EXPERT = 3 * D_MODEL // 8
SWIGLU_LIMIT = 7.0
SWIGLU_ALPHA = 1.702
EXPERT_BLOCK = 256
RMS_EPS = 1e-6
LN_EPS = 1e-5
NEG_INF = -1e30

kernel_name = 'hymba_diffattn_conformer_moe_layer'


def rms_norm(x, g):
    xf = x.astype(jnp.float32)
    y = xf * lax.rsqrt(jnp.mean(xf * xf, axis=-1, keepdims=True) + RMS_EPS)
    return (y * g.astype(jnp.float32)).astype(x.dtype)


def layer_norm(x, g, b):
    xf = x.astype(jnp.float32)
    mu = jnp.mean(xf, axis=-1, keepdims=True)
    var = jnp.mean(jnp.square(xf - mu), axis=-1, keepdims=True)
    y = (xf - mu) * lax.rsqrt(var + LN_EPS)
    return (y * g.astype(jnp.float32) + b.astype(jnp.float32)).astype(x.dtype)


def diff_attention(q, k, v, lam, lam_init, g_subln):
    B, S, _ = q.shape
    H, d = DIFF_HEADS, DIFF_HEAD_DIM
    nb = S // Q_BLOCK
    qb_all = q.reshape(B, nb, Q_BLOCK, H, 2, d).transpose(1, 0, 2, 3, 4, 5)
    k = k.reshape(B, S, H, 2, d)
    v = v.reshape(B, S, H, 2 * d)
    slopes = jnp.exp2(-8.0 * jnp.arange(1, H + 1, dtype=jnp.float32) / H)
    pos_k = jnp.arange(S, dtype=jnp.int32)
    scale = d ** -0.5

    def block(args):
        i, qb = args
        pos_q = i * Q_BLOCK + jnp.arange(Q_BLOCK, dtype=jnp.int32)
        dist = (pos_q[:, None] - pos_k[None, :]).astype(jnp.float32)
        bias = -slopes[:, None, None] * dist
        s = jnp.einsum('bqhcd,bkhcd->bhcqk', qb, k).astype(jnp.float32) * scale
        s = jnp.where(dist >= 0, s + bias[None, :, None], NEG_INF)
        p = jax.nn.softmax(s, axis=-1)
        a = p[:, :, 0] - lam * p[:, :, 1]
        return jnp.einsum('bhqk,bkhe->bqhe', a.astype(v.dtype), v)

    o = lax.map(block, (jnp.arange(nb, dtype=jnp.int32), qb_all))
    o = o.transpose(1, 0, 2, 3, 4).reshape(B, S, H, 2 * d)
    o = rms_norm(o, g_subln) * (1.0 - lam_init)
    return o.reshape(B, S, H * 2 * d)


def conformer_conv(u, w_dw, b_dw, g_ln, b_ln, w_pw):
    glu = u[..., :CONV_CH] * jax.nn.sigmoid(u[..., CONV_CH:])
    c = lax.conv_general_dilated(glu, w_dw, window_strides=(1,), padding=[(CONV_WIDTH - 1, 0)],
                                 dimension_numbers=('NWC', 'WIO', 'NWC'),
                                 feature_group_count=CONV_CH) + b_dw
    c = layer_norm(c, g_ln, b_ln)
    return jax.nn.silu(c) @ w_pw


def memory_cross_attention(h, mem_n, w_cq, w_ck, w_cv, w_co):
    B, S, _ = h.shape
    M = mem_n.shape[1]
    q = (h @ w_cq).reshape(B, S, X_HEADS, X_HEAD_DIM)
    k = (mem_n @ w_ck).reshape(B, M, X_HEADS, X_HEAD_DIM)
    v = (mem_n @ w_cv).reshape(B, M, X_HEADS, X_HEAD_DIM)
    s = jnp.einsum('bqhd,bmhd->bhqm', q, k).astype(jnp.float32) * (X_HEAD_DIM ** -0.5)
    p = jax.nn.softmax(s, axis=-1)
    o = jnp.einsum('bhqm,bmhd->bqhd', p.astype(v.dtype), v).reshape(B, S, X_HEADS * X_HEAD_DIM)
    return o @ w_co


def moe_ffn(h2d, w_router, b_router, w_gate, b_gate, w_up, b_up, w_down, b_down):
    n_tok, d = h2d.shape
    logits = (h2d @ w_router).astype(jnp.float32) + b_router.astype(jnp.float32)
    top_logit, top_e = lax.top_k(logits, TOP_K)
    gate_w = jax.nn.softmax(top_logit, axis=-1)
    n_slot = n_tok * TOP_K
    slot_e = top_e.reshape(n_slot).astype(jnp.int32)
    slot_tok = jnp.arange(n_slot, dtype=jnp.int32) // TOP_K
    slot_w = gate_w.reshape(n_slot)
    order = jnp.argsort(slot_e, stable=True)
    se, st, sw = slot_e[order], slot_tok[order], slot_w[order]
    counts = jnp.bincount(slot_e, length=N_EXPERTS).astype(jnp.int32)
    padded = (counts + EXPERT_BLOCK - 1) // EXPERT_BLOCK * EXPERT_BLOCK
    pad_end = jnp.cumsum(padded)
    pad_start = pad_end - padded
    start = jnp.cumsum(counts) - counts
    dest = pad_start[se] + jnp.arange(n_slot, dtype=jnp.int32) - start[se]
    n_blocks = -(-n_slot // EXPERT_BLOCK) + N_EXPERTS
    n_rows = n_blocks * EXPERT_BLOCK
    row_tok = jnp.full((n_rows,), n_tok, jnp.int32).at[dest].set(st)
    row_w = jnp.zeros((n_rows,), jnp.float32).at[dest].set(sw)
    block_e = jnp.minimum(jnp.searchsorted(pad_end, jnp.arange(n_blocks, dtype=jnp.int32) * EXPERT_BLOCK,
                                           side='right'), N_EXPERTS - 1).astype(jnp.int32)
    h_pad = jnp.concatenate([h2d, jnp.zeros((1, d), h2d.dtype)], axis=0)

    def step(acc, blk):
        toks, wts, e = blk
        xe = h_pad[toks]
        g = jnp.minimum(xe @ w_gate[e] + b_gate[e], SWIGLU_LIMIT)
        u = jnp.clip(xe @ w_up[e] + b_up[e], -SWIGLU_LIMIT, SWIGLU_LIMIT)
        act = (u + 1.0) * (g * jax.nn.sigmoid(SWIGLU_ALPHA * g))
        y = act @ w_down[e] + b_down[e]
        acc = acc.at[toks].add(y.astype(jnp.float32) * wts[:, None])
        return acc, None

    acc0 = jnp.zeros((n_tok + 1, d), jnp.float32)
    acc, _ = lax.scan(step, acc0, (row_tok.reshape(n_blocks, EXPERT_BLOCK),
                                   row_w.reshape(n_blocks, EXPERT_BLOCK), block_e))
    return acc[:n_tok].astype(h2d.dtype)


def setup_inputs(seed: int = 0) -> dict:
    key = jax.random.key(seed)
    ks = jax.random.split(key, 32)
    L, D, C, d, E, F = DEPTH, D_MODEL, CONV_CH, DIFF_HEAD_DIM, N_EXPERTS, D_EXPERT

    def nrm(k, shape, scale):
        return jax.random.normal(k, shape, jnp.float32) * scale

    def gain(k, shape):
        return 1.0 + 0.02 * jax.random.normal(k, shape, jnp.float32)

    return {
        'x': nrm(ks[0], (BATCH, SEQ, D), 1.0),
        'mem': nrm(ks[1], (BATCH, MEM_LEN, D), 1.0),
        'g_mix': gain(ks[2], (L, D)),
        'w_in': nrm(ks[3], (L, D, IN_COLS), D ** -0.5),
        'lambda_q1': nrm(ks[4], (L, d), 0.1),
        'lambda_k1': nrm(ks[5], (L, d), 0.1),
        'lambda_q2': nrm(ks[6], (L, d), 0.1),
        'lambda_k2': nrm(ks[7], (L, d), 0.1),
        'g_subln': gain(ks[8], (L, 2 * d)),
        'w_dw': nrm(ks[9], (L, CONV_WIDTH, 1, C), CONV_WIDTH ** -0.5),
        'b_dw': nrm(ks[10], (L, C), 0.01),
        'g_conv_ln': gain(ks[11], (L, C)),
        'b_conv_ln': nrm(ks[12], (L, C), 0.01),
        'w_conv_pw': nrm(ks[13], (L, C, C), C ** -0.5),
        'w_out': nrm(ks[14], (L, MIX_WIDTH, D), MIX_WIDTH ** -0.5),
        'g_cross': gain(ks[15], (L, D)),
        'g_mem': gain(ks[16], (L, D)),
        'w_cq': nrm(ks[17], (L, D, D), D ** -0.5),
        'w_ck': nrm(ks[18], (L, D, D), D ** -0.5),
        'w_cv': nrm(ks[19], (L, D, D), D ** -0.5),
        'w_co': nrm(ks[20], (L, D, D), D ** -0.5),
        'g_ffn': gain(ks[21], (L, D)),
        'w_router': nrm(ks[22], (L, D, E), D ** -0.5),
        'b_router': nrm(ks[23], (L, E), 0.01),
        'w_gate': nrm(ks[24], (L, E, D, F), D ** -0.5),
        'b_gate': nrm(ks[25], (L, E, F), 0.01),
        'w_up': nrm(ks[26], (L, E, D, F), D ** -0.5),
        'b_up': nrm(ks[27], (L, E, F), 0.01),
        'w_down': nrm(ks[28], (L, E, F, D), F ** -0.5),
        'b_down': nrm(ks[29], (L, E, D), 0.01),
        'g_final': gain(ks[30], (D,)),
    }


def reference(x, mem, g_mix, w_in, lambda_q1, lambda_k1, lambda_q2, lambda_k2, g_subln,
              w_dw, b_dw, g_conv_ln, b_conv_ln, w_conv_pw, w_out, g_cross, g_mem,
              w_cq, w_ck, w_cv, w_co, g_ffn, w_router, b_router, w_gate, b_gate,
              w_up, b_up, w_down, b_down, g_final):
    B, S, D = x.shape
    for l in range(DEPTH):
        h = rms_norm(x, g_mix[l])
        proj = h @ w_in[l]
        q = proj[..., :DIFF_WIDTH]
        k = proj[..., DIFF_WIDTH:2 * DIFF_WIDTH]
        v = proj[..., 2 * DIFF_WIDTH:3 * DIFF_WIDTH]
        u = proj[..., 3 * DIFF_WIDTH:]
        lam_init = 0.8 - 0.6 * math.exp(-0.3 * l)
        lam = (jnp.exp(jnp.sum(lambda_q1[l].astype(jnp.float32) * lambda_k1[l].astype(jnp.float32)))
               - jnp.exp(jnp.sum(lambda_q2[l].astype(jnp.float32) * lambda_k2[l].astype(jnp.float32)))
               + lam_init)
        a_out = diff_attention(q, k, v, lam, lam_init, g_subln[l])
        c_out = conformer_conv(u, w_dw[l], b_dw[l], g_conv_ln[l], b_conv_ln[l], w_conv_pw[l])
        x = x + jnp.concatenate([a_out, c_out], axis=-1) @ w_out[l]
        x = x + memory_cross_attention(rms_norm(x, g_cross[l]), rms_norm(mem, g_mem[l]),
                                       w_cq[l], w_ck[l], w_cv[l], w_co[l])
        h2d = rms_norm(x, g_ffn[l]).reshape(B * S, D)
        x = x + moe_ffn(h2d, w_router[l], b_router[l], w_gate[l], b_gate[l],
                        w_up[l], b_up[l], w_down[l], b_down[l]).reshape(B, S, D)
    return rms_norm(x, g_final)
```

```python
import functools
import math
from typing import NamedTuple

import jax
import jax.numpy as jnp
from jax import lax
from jax.experimental import pallas as pl
from jax.experimental.pallas import tpu as pltpu

f32 = jnp.float32
bf16 = jnp.bfloat16

RMS_EPS = 1e-6
LN_EPS = 1e-5
NEG_INF = -1e30
SWIGLU_LIMIT = 7.0
SWIGLU_ALPHA = 1.702

LANES = 128
VMEM_LIMIT_BYTES = 56 * 1024 * 1024


class Cfg(NamedTuple):
    diff_heads: int = 8
    conv_width: int = 31
    x_heads: int = 4
    top_k: int = 4


class Tiles(NamedTuple):
    norm_rows: int = 256
    mm_rows: int = 1024
    mm_cols: int = 1024
    mm_cols_res: int = 512
    attn_q: int = 512
    conv_rows: int = 256
    conv_halo: int = 32
    xattn_q: int = 512
    router_rows: int = 256
    expert_rows: int = 256
    expert_cols: int = 512
    combine_rows: int = 128


def _params(*sem):
    return pltpu.CompilerParams(dimension_semantics=sem, vmem_limit_bytes=VMEM_LIMIT_BYTES)


def _rmsnorm_kernel(x_ref, g_ref, o_ref):
    x = x_ref[...]
    ms = jnp.mean(x * x, axis=-1, keepdims=True)
    o_ref[...] = (x * lax.rsqrt(ms + RMS_EPS) * g_ref[...]).astype(o_ref.dtype)


def _rmsnorm(x2d, g, rows):
    n, d = x2d.shape
    rows = min(rows, n)
    return pl.pallas_call(
        _rmsnorm_kernel,
        grid=(n // rows,),
        in_specs=[pl.BlockSpec((rows, d), lambda i: (i, 0)),
                  pl.BlockSpec((1, d), lambda i: (0, 0))],
        out_specs=pl.BlockSpec((rows, d), lambda i: (i, 0)),
        out_shape=jax.ShapeDtypeStruct((n, d), bf16),
        compiler_params=_params("parallel"),
        name="rmsnorm",
    )(x2d, g.reshape(1, d).astype(f32))


def _matmul_kernel(*refs, n_lhs, has_res):
    a_refs = refs[:n_lhs]
    w_refs = refs[n_lhs:2 * n_lhs]
    o_ref = refs[-1]
    acc = jnp.dot(a_refs[0][...], w_refs[0][...], preferred_element_type=f32)
    for p in range(1, n_lhs):
        acc = acc + jnp.dot(a_refs[p][...], w_refs[p][...], preferred_element_type=f32)
    if has_res:
        acc = acc + refs[2 * n_lhs][...]
    o_ref[...] = acc.astype(o_ref.dtype)


def _matmul(lhs_list, w, res, out_dtype, rows, cols, name):
    n_lhs = len(lhs_list)
    m, kp = lhs_list[0].shape
    nc = w.shape[1]
    assert w.shape[0] == n_lhs * kp
    rows = min(rows, m)
    cols = min(cols, nc)
    in_specs = [pl.BlockSpec((rows, kp), lambda i, j: (i, 0)) for _ in range(n_lhs)]
    in_specs += [pl.BlockSpec((kp, cols), lambda i, j, p=p: (p, j)) for p in range(n_lhs)]
    args = list(lhs_list) + [w] * n_lhs
    if res is not None:
        in_specs.append(pl.BlockSpec((rows, cols), lambda i, j: (i, j)))
        args.append(res)
    return pl.pallas_call(
        functools.partial(_matmul_kernel, n_lhs=n_lhs, has_res=res is not None),
        grid=(m // rows, nc // cols),
        in_specs=in_specs,
        out_specs=pl.BlockSpec((rows, cols), lambda i, j: (i, j)),
        out_shape=jax.ShapeDtypeStruct((m, nc), out_dtype),
        compiler_params=_params("parallel", "parallel"),
        name=name,
    )(*args)


def _diff_attn_kernel(slopes_ref, q_ref, k_ref, v_ref, lq1_ref, lk1_ref, lq2_ref, lk2_ref, g_ref,
                      o_ref, acc1_ref, acc2_ref, m1_ref, l1_ref, m2_ref, l2_ref, *, tq, d, lam_init):
    h = pl.program_id(1)
    qi = pl.program_id(2)
    slope = slopes_ref[h]
    scale = d ** -0.5
    rows = lax.broadcasted_iota(jnp.int32, (tq, tq), 0)
    cols = lax.broadcasted_iota(jnp.int32, (tq, tq), 1)
    dij = rows - cols
    bias = -slope * dij.astype(f32)
    q = q_ref[...]
    qs = (q[:, :d], q[:, d:])
    stats = ((m1_ref, l1_ref, acc1_ref), (m2_ref, l2_ref, acc2_ref))
    for m_ref, l_ref, acc_ref in stats:
        m_ref[...] = jnp.full(m_ref.shape, NEG_INF, f32)
        l_ref[...] = jnp.zeros(l_ref.shape, f32)
        acc_ref[...] = jnp.zeros(acc_ref.shape, f32)

    def block(j, masked):
        k0 = pl.multiple_of(j * tq, tq)
        kb = k_ref[pl.ds(k0, tq), :]
        vb = v_ref[pl.ds(k0, tq), :]
        cj = -slope * ((qi - j) * tq).astype(f32)
        for c, (m_ref, l_ref, acc_ref) in enumerate(stats):
            s = lax.dot_general(qs[c], kb[:, c * d:(c + 1) * d], (((1,), (1,)), ((), ())),
                                preferred_element_type=f32) * scale + bias
            if masked:
                s = jnp.where(dij >= 0, s, NEG_INF)
            m_old = m_ref[...]
            m_new = jnp.maximum(m_old, jnp.max(s, axis=-1, keepdims=True) + cj)
            p = jnp.exp(s - (m_new - cj))
            alpha = jnp.exp(m_old - m_new)
            l_ref[...] = alpha * l_ref[...] + jnp.sum(p, axis=-1, keepdims=True)
            acc_ref[...] = alpha * acc_ref[...] + jnp.dot(p.astype(bf16), vb,
                                                          preferred_element_type=f32)
            m_ref[...] = m_new

    def off_diag(j, carry):
        block(j, False)
        return carry

    lax.fori_loop(0, qi, off_diag, 0)
    block(qi, True)

    lam = (jnp.exp(jnp.sum(lq1_ref[...] * lk1_ref[...], axis=-1, keepdims=True))
           - jnp.exp(jnp.sum(lq2_ref[...] * lk2_ref[...], axis=-1, keepdims=True)) + lam_init)
    o = acc1_ref[...] / l1_ref[...] - lam * (acc2_ref[...] / l2_ref[...])
    ms = jnp.mean(o * o, axis=-1, keepdims=True)
    y = o * lax.rsqrt(ms + RMS_EPS) * g_ref[...] * (1.0 - lam_init)
    o_ref[...] = y.astype(o_ref.dtype)


def _diff_attention(proj3, lq1, lk1, lq2, lk2, g_subln, lam_init, cfg, tiles):
    b, s, _ = proj3.shape
    nh = cfg.diff_heads
    d = lq1.shape[-1]
    hw = 2 * d
    tq = min(tiles.attn_q, s)
    slopes = jnp.exp2(-8.0 * jnp.arange(1, nh + 1, dtype=f32) / nh)
    vec = lambda a: a.reshape(1, -1).astype(f32)
    small = lambda n: pl.BlockSpec((1, n), lambda bi, hi, qi: (0, 0))
    return pl.pallas_call(
        functools.partial(_diff_attn_kernel, tq=tq, d=d, lam_init=lam_init),
        grid=(b, nh, s // tq),
        in_specs=[pl.BlockSpec(memory_space=pltpu.SMEM),
                  pl.BlockSpec((None, tq, hw), lambda bi, hi, qi: (bi, qi, hi)),
                  pl.BlockSpec((None, s, hw), lambda bi, hi, qi: (bi, 0, nh + hi)),
                  pl.BlockSpec((None, s, hw), lambda bi, hi, qi: (bi, 0, 2 * nh + hi)),
                  small(d), small(d), small(d), small(d), small(hw)],
        out_specs=pl.BlockSpec((None, tq, hw), lambda bi, hi, qi: (bi, qi, hi)),
        out_shape=jax.ShapeDtypeStruct((b, s, nh * hw), bf16),
        scratch_shapes=[pltpu.VMEM((tq, hw), f32), pltpu.VMEM((tq, hw), f32),
                        pltpu.VMEM((tq, 1), f32), pltpu.VMEM((tq, 1), f32),
                        pltpu.VMEM((tq, 1), f32), pltpu.VMEM((tq, 1), f32)],
        compiler_params=_params("parallel", "parallel", "parallel"),
        name="diff_attention",
    )(slopes, proj3, proj3, proj3, vec(lq1), vec(lk1), vec(lq2), vec(lk2), vec(g_subln))


def _conformer_kernel(a_ref, gate_ref, ah_ref, gateh_ref, wdw_ref, bdw_ref, gln_ref, bln_ref,
                      wpw_ref, o_ref, buf_ref, conv_ref, *, ts, kw, halo):
    i = pl.program_id(1)
    ch = a_ref.shape[-1]
    glu = a_ref[...].astype(f32) * jax.nn.sigmoid(gate_ref[...].astype(f32))
    glu_h = ah_ref[...].astype(f32) * jax.nn.sigmoid(gateh_ref[...].astype(f32))
    buf_ref[0:halo, :] = jnp.where(i > 0, glu_h, 0.0)
    buf_ref[halo:halo + ts, :] = glu
    off = halo - (kw - 1)
    for c in range(ch // LANES):
        cs = slice(c * LANES, (c + 1) * LANES)
        acc = jnp.broadcast_to(bdw_ref[:, cs], (ts, LANES))
        for j in range(kw):
            acc = acc + wdw_ref[j:j + 1, cs] * buf_ref[off + j:off + j + ts, cs]
        conv_ref[:, cs] = acc
    c = conv_ref[...]
    mu = jnp.mean(c, axis=-1, keepdims=True)
    xc = c - mu
    var = jnp.mean(xc * xc, axis=-1, keepdims=True)
    y = xc * lax.rsqrt(var + LN_EPS) * gln_ref[...] + bln_ref[...]
    act = y * jax.nn.sigmoid(y)
    o_ref[...] = jnp.dot(act.astype(bf16), wpw_ref[...],
                         preferred_element_type=f32).astype(o_ref.dtype)


def _conformer(proj3, col0, w_dw, b_dw, g_ln, b_ln, w_pw, cfg, tiles):
    b, s, _ = proj3.shape
    kw = cfg.conv_width
    ch = w_pw.shape[0]
    ts = min(tiles.conv_rows, s)
    halo = tiles.conv_halo
    assert halo >= kw - 1 and ts % halo == 0 and col0 % ch == 0
    cb = col0 // ch
    hb = ts // halo
    vec = lambda a: a.reshape(1, ch).astype(f32)
    small = lambda: pl.BlockSpec((1, ch), lambda bi, i: (0, 0))
    return pl.pallas_call(
        functools.partial(_conformer_kernel, ts=ts, kw=kw, halo=halo),
        grid=(b, s // ts),
        in_specs=[pl.BlockSpec((None, ts, ch), lambda bi, i: (bi, i, cb)),
                  pl.BlockSpec((None, ts, ch), lambda bi, i: (bi, i, cb + 1)),
                  pl.BlockSpec((None, halo, ch), lambda bi, i: (bi, jnp.maximum(i * hb - 1, 0), cb)),
                  pl.BlockSpec((None, halo, ch), lambda bi, i: (bi, jnp.maximum(i * hb - 1, 0), cb + 1)),
                  pl.BlockSpec((kw, ch), lambda bi, i: (0, 0)),
                  small(), small(), small(),
                  pl.BlockSpec((ch, ch), lambda bi, i: (0, 0))],
        out_specs=pl.BlockSpec((None, ts, ch), lambda bi, i: (bi, i, 0)),
        out_shape=jax.ShapeDtypeStruct((b, s, ch), bf16),
        scratch_shapes=[pltpu.VMEM((halo + ts, ch), f32), pltpu.VMEM((ts, ch), f32)],
        compiler_params=_params("parallel", "parallel"),
        name="conformer",
    )(proj3, proj3, proj3, proj3, w_dw.reshape(kw, ch).astype(f32), vec(b_dw), vec(g_ln), vec(b_ln),
      w_pw)


def _xattn_kernel(q_ref, k_ref, v_ref, o_ref, *, scale):
    s = lax.dot_general(q_ref[...], k_ref[...], (((1,), (1,)), ((), ())),
                        preferred_element_type=f32) * scale
    m = jnp.max(s, axis=-1, keepdims=True)
    p = jnp.exp(s - m)
    p = p / jnp.sum(p, axis=-1, keepdims=True)
    o_ref[...] = jnp.dot(p.astype(bf16), v_ref[...], preferred_element_type=f32).astype(o_ref.dtype)


def _cross_attention(q3, k3, v3, cfg, tiles):
    b, s, dm = q3.shape
    m = k3.shape[1]
    nh = cfg.x_heads
    hd = dm // nh
    tq = min(tiles.xattn_q, s)
    return pl.pallas_call(
        functools.partial(_xattn_kernel, scale=hd ** -0.5),
        grid=(b, s // tq, nh),
        in_specs=[pl.BlockSpec((None, tq, hd), lambda bi, i, h: (bi, i, h)),
                  pl.BlockSpec((None, m, hd), lambda bi, i, h: (bi, 0, h)),
                  pl.BlockSpec((None, m, hd), lambda bi, i, h: (bi, 0, h))],
        out_specs=pl.BlockSpec((None, tq, hd), lambda bi, i, h: (bi, i, h)),
        out_shape=jax.ShapeDtypeStruct((b, s, dm), bf16),
        compiler_params=_params("parallel", "parallel", "parallel"),
        name="cross_attention",
    )(q3, k3, v3)


def _router_kernel(x_ref, g_ref, wr_ref, br_ref, hg_ref, te_ref, tw_ref, *, tm, n_sub, top_k):
    x = x_ref[...]
    ms = jnp.mean(x * x, axis=-1, keepdims=True)
    h = x * lax.rsqrt(ms + RMS_EPS) * g_ref[...]
    for s in range(n_sub):
        hg_ref[pl.ds(s, tm, stride=n_sub), :] = h[:, s * LANES:(s + 1) * LANES]
    logits = jnp.dot(h.astype(bf16), wr_ref[...], preferred_element_type=f32) + br_ref[...]
    lane = lax.broadcasted_iota(jnp.int32, logits.shape, 1)
    lane_f = lane.astype(f32)
    vals, idxs = [], []
    for _ in range(top_k):
        m = jnp.max(logits, axis=-1, keepdims=True)
        idx = jnp.min(jnp.where(logits == m, lane_f, float(LANES)), axis=-1, keepdims=True)
        vals.append(m)
        idxs.append(idx)
        logits = jnp.where(lane_f == idx, NEG_INF * 2.0, logits)
    exps = [jnp.exp(v - vals[0]) for v in vals]
    denom = exps[0]
    for e in exps[1:]:
        denom = denom + e
    te = jnp.zeros(logits.shape, f32)
    tw = jnp.zeros(logits.shape, f32)
    for k in range(top_k):
        te = jnp.where(lane == k, idxs[k], te)
        tw = jnp.where(lane == k, exps[k] / denom, tw)
    te_ref[...] = te.astype(jnp.int32)
    tw_ref[...] = tw


def _norm_router(x2d, g, w_router, b_router, cfg, tiles):
    n, d = x2d.shape
    e = w_router.shape[1]
    assert e <= LANES and d % LANES == 0
    n_sub = d // LANES
    tm = min(tiles.router_rows, n)
    wr = jnp.zeros((d, LANES), bf16).at[:, :e].set(w_router.astype(bf16))
    br = jnp.full((1, LANES), NEG_INF, f32).at[0, :e].set(b_router.astype(f32))
    hg, te, tw = pl.pallas_call(
        functools.partial(_router_kernel, tm=tm, n_sub=n_sub, top_k=cfg.top_k),
        grid=(n // tm,),
        in_specs=[pl.BlockSpec((tm, d), lambda i: (i, 0)),
                  pl.BlockSpec((1, d), lambda i: (0, 0)),
                  pl.BlockSpec((d, LANES), lambda i: (0, 0)),
                  pl.BlockSpec((1, LANES), lambda i: (0, 0))],
        out_specs=[pl.BlockSpec((tm * n_sub, LANES), lambda i: (i, 0)),
                   pl.BlockSpec((tm, LANES), lambda i: (i, 0)),
                   pl.BlockSpec((tm, LANES), lambda i: (i, 0))],
        out_shape=[jax.ShapeDtypeStruct((n * n_sub, LANES), f32),
                   jax.ShapeDtypeStruct((n, LANES), jnp.int32),
                   jax.ShapeDtypeStruct((n, LANES), f32)],
        compiler_params=_params("parallel"),
        name="norm_router",
    )(x2d, g.reshape(1, d).astype(f32), wr, br)
    return hg, te[:, :cfg.top_k], tw[:, :cfg.top_k]


def _routing_plan(top_e, gate_w, n_experts, blk):
    n_tok, top_k = top_e.shape
    n_slot = n_tok * top_k
    slot_e = top_e.reshape(n_slot)
    slot_tok = jnp.arange(n_slot, dtype=jnp.int32) // top_k
    slot_w = gate_w.reshape(n_slot)
    order = jnp.argsort(slot_e, stable=True)
    se = slot_e[order]
    counts = jnp.bincount(slot_e, length=n_experts).astype(jnp.int32)
    padded = (counts + blk - 1) // blk * blk
    pad_end = jnp.cumsum(padded)
    pad_start = pad_end - padded
    start = jnp.cumsum(counts) - counts
    dest = pad_start[se] + jnp.arange(n_slot, dtype=jnp.int32) - start[se]
    n_blocks = -(-n_slot // blk) + n_experts
    n_rows = n_blocks * blk
    row_tok = jnp.zeros((n_rows,), jnp.int32).at[dest].set(slot_tok[order])
    row_w = jnp.zeros((n_rows,), f32).at[dest].set(slot_w[order])
    block_e = jnp.minimum(jnp.searchsorted(pad_end, jnp.arange(n_blocks, dtype=jnp.int32) * blk,
                                           side='right'), n_experts - 1).astype(jnp.int32)
    pos = jnp.zeros((n_slot,), jnp.int32).at[order].set(dest)
    return row_tok, row_w.reshape(n_rows, 1), block_e, pos


def _expert_kernel(be_ref, tok_ref, hg_hbm, wg_ref, bg_ref, wu_ref, bu_ref, wd_ref, bd_ref, rw_ref,
                   o_ref, g0_ref, g1_ref, x_ref, acc_ref, sem, *, blk, n_sub, nf):
    i = pl.program_id(0)
    f = pl.program_id(1)
    nb = pl.num_programs(0)
    gbufs = (g0_ref, g1_ref)

    def row_copy(block_idx, r, slot):
        tok = tok_ref[block_idx * blk + r]
        return pltpu.make_async_copy(
            hg_hbm.at[pl.ds(pl.multiple_of(tok * n_sub, n_sub), n_sub)],
            gbufs[slot].at[pl.ds(pl.multiple_of(r * n_sub, n_sub), n_sub)],
            sem.at[slot])

    def issue(block_idx, slot):
        def body(r, carry):
            row_copy(block_idx, r, slot).start()
            return carry
        lax.fori_loop(0, blk, body, 0)

    def wait_all(slot):
        pltpu.make_async_copy(hg_hbm.at[pl.ds(0, blk * n_sub)], gbufs[slot], sem.at[slot]).wait()

    @pl.when(f == 0)
    def _():
        @pl.when(i == 0)
        def _():
            issue(0, 0)
        for slot in range(2):
            @pl.when(i % 2 == slot)
            def _():
                @pl.when(i + 1 < nb)
                def _():
                    issue(i + 1, 1 - slot)
                wait_all(slot)
                for s in range(n_sub):
                    x_ref[:, s * LANES:(s + 1) * LANES] = (
                        gbufs[slot][pl.ds(s, blk, stride=n_sub), :].astype(bf16))

    x = x_ref[...]
    g = jnp.minimum(jnp.dot(x, wg_ref[...], preferred_element_type=f32) + bg_ref[...], SWIGLU_LIMIT)
    u = jnp.clip(jnp.dot(x, wu_ref[...], preferred_element_type=f32) + bu_ref[...],
                 -SWIGLU_LIMIT, SWIGLU_LIMIT)
    act = (u + 1.0) * (g * jax.nn.sigmoid(SWIGLU_ALPHA * g))
    y = jnp.dot(act.astype(bf16), wd_ref[...], preferred_element_type=f32)

    @pl.when(f == 0)
    def _():
        acc_ref[...] = y + bd_ref[...]

    @pl.when(f > 0)
    def _():
        acc_ref[...] += y

    @pl.when(f == nf - 1)
    def _():
        yw = acc_ref[...] * rw_ref[...]
        for s in range(n_sub):
            o_ref[pl.ds(s, blk, stride=n_sub), :] = yw[:, s * LANES:(s + 1) * LANES]


def _experts(hg, row_tok, row_w, block_e, wg, bg, wu, bu, wd, bd, tiles):
    n_e, d, ff = wg.shape
    n_sub = d // LANES
    blk = tiles.expert_rows
    n_rows = row_tok.shape[0]
    n_blocks = n_rows // blk
    tf = min(tiles.expert_cols, ff)
    nf = ff // tf
    grid_spec = pltpu.PrefetchScalarGridSpec(
        num_scalar_prefetch=2,
        grid=(n_blocks, nf),
        in_specs=[pl.BlockSpec(memory_space=pl.ANY),
                  pl.BlockSpec((None, d, tf), lambda i, f, be, tok: (be[i], 0, f)),
                  pl.BlockSpec((None, 1, tf), lambda i, f, be, tok: (be[i], 0, f)),
                  pl.BlockSpec((None, d, tf), lambda i, f, be, tok: (be[i], 0, f)),
                  pl.BlockSpec((None, 1, tf), lambda i, f, be, tok: (be[i], 0, f)),
                  pl.BlockSpec((None, tf, d), lambda i, f, be, tok: (be[i], f, 0)),
                  pl.BlockSpec((None, 1, d), lambda i, f, be, tok: (be[i], 0, 0)),
                  pl.BlockSpec((blk, 1), lambda i, f, be, tok: (i, 0))],
        out_specs=pl.BlockSpec((blk * n_sub, LANES), lambda i, f, be, tok: (i, 0)),
        scratch_shapes=[pltpu.VMEM((blk * n_sub, LANES), f32), pltpu.VMEM((blk * n_sub, LANES), f32),
                        pltpu.VMEM((blk, d), bf16), pltpu.VMEM((blk, d), f32),
                        pltpu.SemaphoreType.DMA((2,))],
    )
    return pl.pallas_call(
        functools.partial(_expert_kernel, blk=blk, n_sub=n_sub, nf=nf),
        grid_spec=grid_spec,
        out_shape=jax.ShapeDtypeStruct((n_rows * n_sub, LANES), f32),
        compiler_params=_params("arbitrary", "arbitrary"),
        name="experts",
    )(block_e, row_tok, hg, wg, bg.reshape(n_e, 1, ff).astype(f32), wu,
      bu.reshape(n_e, 1, ff).astype(f32), wd, bd.reshape(n_e, 1, d).astype(f32), row_w)


def _combine_kernel(pos_ref, x_ref, ys_hbm, g_ref, o_ref, b0_ref, b1_ref, sem, *, tm, n_sub, top_k,
                    final_norm):
    i = pl.program_id(0)
    nb = pl.num_programs(0)
    bufs = (b0_ref, b1_ref)
    n_copy = tm * top_k

    def row_copy(block_idx, r, slot):
        k = r // tm
        t = r - k * tm
        row = pos_ref[(block_idx * tm + t) * top_k + k]
        return pltpu.make_async_copy(
            ys_hbm.at[pl.ds(pl.multiple_of(row * n_sub, n_sub), n_sub)],
            bufs[slot].at[pl.ds(pl.multiple_of(r * n_sub, n_sub), n_sub)],
            sem.at[slot])

    def issue(block_idx, slot):
        def body(r, carry):
            row_copy(block_idx, r, slot).start()
            return carry
        lax.fori_loop(0, n_copy, body, 0)

    def wait_all(slot):
        pltpu.make_async_copy(ys_hbm.at[pl.ds(0, n_copy * n_sub)], bufs[slot], sem.at[slot]).wait()

    @pl.when(i == 0)
    def _():
        issue(0, 0)

    for slot in range(2):
        @pl.when(i % 2 == slot)
        def _():
            @pl.when(i + 1 < nb)
            def _():
                issue(i + 1, 1 - slot)
            wait_all(slot)
            for s in range(n_sub):
                cs = slice(s * LANES, (s + 1) * LANES)
                acc = x_ref[:, cs]
                for k in range(top_k):
                    acc = acc + bufs[slot][pl.ds(k * tm * n_sub + s, tm, stride=n_sub), :]
                o_ref[:, cs] = acc

    if final_norm:
        x = o_ref[...]
        ms = jnp.mean(x * x, axis=-1, keepdims=True)
        o_ref[...] = x * lax.rsqrt(ms + RMS_EPS) * g_ref[...]


def _combine(x2d, ys, pos, g_final, final_norm, cfg, tiles):
    n, d = x2d.shape
    n_sub = d // LANES
    tm = min(tiles.combine_rows, n)
    top_k = cfg.top_k
    grid_spec = pltpu.PrefetchScalarGridSpec(
        num_scalar_prefetch=1,
        grid=(n // tm,),
        in_specs=[pl.BlockSpec((tm, d), lambda i, pos: (i, 0)),
                  pl.BlockSpec(memory_space=pl.ANY),
                  pl.BlockSpec((1, d), lambda i, pos: (0, 0))],
        out_specs=pl.BlockSpec((tm, d), lambda i, pos: (i, 0)),
        scratch_shapes=[pltpu.VMEM((top_k * tm * n_sub, LANES), f32),
                        pltpu.VMEM((top_k * tm * n_sub, LANES), f32),
                        pltpu.SemaphoreType.DMA((2,))],
    )
    return pl.pallas_call(
        functools.partial(_combine_kernel, tm=tm, n_sub=n_sub, top_k=top_k, final_norm=final_norm),
        grid_spec=grid_spec,
        out_shape=jax.ShapeDtypeStruct((n, d), f32),
        compiler_params=_params("arbitrary"),
        name="moe_combine",
    )(pos, x2d, ys, g_final.reshape(1, d).astype(f32))


def _layer(x, mem, p, l, is_last, g_final, cfg, tiles):
    b, s, d = x.shape
    n = b * s
    m_len = mem.shape[1]
    diff_width = d // 2
    conv_ch = d - diff_width
    x2d = x.reshape(n, d)

    h = _rmsnorm(x2d, p['g_mix'][l], tiles.norm_rows)
    proj = _matmul([h], p['w_in'][l].astype(bf16), None, bf16, tiles.mm_rows, tiles.mm_cols, "in_proj")
    proj3 = proj.reshape(b, s, -1)
    lam_init = 0.8 - 0.6 * math.exp(-0.3 * l)
    a_out = _diff_attention(proj3, p['lambda_q1'][l], p['lambda_k1'][l], p['lambda_q2'][l],
                            p['lambda_k2'][l], p['g_subln'][l], lam_init, cfg, tiles)
    c_out = _conformer(proj3, 3 * diff_width, p['w_dw'][l], p['b_dw'][l], p['g_conv_ln'][l],
                       p['b_conv_ln'][l], p['w_conv_pw'][l].astype(bf16), cfg, tiles)
    assert diff_width == conv_ch
    x2d = _matmul([a_out.reshape(n, diff_width), c_out.reshape(n, conv_ch)],
                  p['w_out'][l].astype(bf16), x2d, f32, tiles.mm_rows, tiles.mm_cols_res, "out_proj")

    hc = _rmsnorm(x2d, p['g_cross'][l], tiles.norm_rows)
    mem_n = _rmsnorm(mem.reshape(b * m_len, d), p['g_mem'][l], tiles.norm_rows)
    qc = _matmul([hc], p['w_cq'][l].astype(bf16), None, bf16, tiles.mm_rows, tiles.mm_cols, "xq_proj")
    kc = _matmul([mem_n], p['w_ck'][l].astype(bf16), None, bf16, tiles.mm_rows, tiles.mm_cols, "xk_proj")
    vc = _matmul([mem_n], p['w_cv'][l].astype(bf16), None, bf16, tiles.mm_rows, tiles.mm_cols, "xv_proj")
    oc = _cross_attention(qc.reshape(b, s, d), kc.reshape(b, m_len, d), vc.reshape(b, m_len, d),
                          cfg, tiles)
    x2d = _matmul([oc.reshape(n, d)], p['w_co'][l].astype(bf16), x2d, f32, tiles.mm_rows,
                  tiles.mm_cols_res, "xo_proj")

    n_experts = p['w_router'].shape[-1]
    hg, top_e, gate_w = _norm_router(x2d, p['g_ffn'][l], p['w_router'][l], p['b_router'][l], cfg, tiles)
    row_tok, row_w, block_e, pos = _routing_plan(top_e, gate_w, n_experts, tiles.expert_rows)
    ys = _experts(hg, row_tok, row_w, block_e, p['w_gate'][l].astype(bf16), p['b_gate'][l],
                  p['w_up'][l].astype(bf16), p['b_up'][l], p['w_down'][l].astype(bf16), p['b_down'][l],
                  tiles)
    x2d = _combine(x2d, ys, pos, g_final, is_last, cfg, tiles)
    return x2d.reshape(b, s, d)


def _forward(x, mem, params, g_final, cfg, tiles):
    depth = params['g_mix'].shape[0]
    for l in range(depth):
        x = _layer(x, mem, params, l, l == depth - 1, g_final, cfg, tiles)
    return x


def kernel(x, mem, g_mix, w_in, lambda_q1, lambda_k1, lambda_q2, lambda_k2, g_subln, w_dw, b_dw,
           g_conv_ln, b_conv_ln, w_conv_pw, w_out, g_cross, g_mem, w_cq, w_ck, w_cv, w_co, g_ffn,
           w_router, b_router, w_gate, b_gate, w_up, b_up, w_down, b_down, g_final):
    params = dict(g_mix=g_mix, w_in=w_in, lambda_q1=lambda_q1, lambda_k1=lambda_k1,
                  lambda_q2=lambda_q2, lambda_k2=lambda_k2, g_subln=g_subln, w_dw=w_dw, b_dw=b_dw,
                  g_conv_ln=g_conv_ln, b_conv_ln=b_conv_ln, w_conv_pw=w_conv_pw, w_out=w_out,
                  g_cross=g_cross, g_mem=g_mem, w_cq=w_cq, w_ck=w_ck, w_cv=w_cv, w_co=w_co,
                  g_ffn=g_ffn, w_router=w_router, b_router=b_router, w_gate=w_gate, b_gate=b_gate,
                  w_up=w_up, b_up=b_up, w_down=w_down, b_down=b_down)
    return _forward(x, mem, params, g_final, Cfg(), Tiles())
```

```python
import functools
import math
from typing import NamedTuple

import jax
import jax.numpy as jnp
from jax import lax
from jax.experimental import pallas as pl
from jax.experimental.pallas import tpu as pltpu

f32 = jnp.float32
bf16 = jnp.bfloat16

RMS_EPS = 1e-6
LN_EPS = 1e-5
NEG_INF = -1e30
SWIGLU_LIMIT = 7.0
SWIGLU_ALPHA = 1.702

LANES = 128
SLAB_PAD = 1
VMEM_LIMIT_BYTES = 56 * 1024 * 1024


class Cfg(NamedTuple):
    diff_heads: int = 8
    conv_width: int = 31
    x_heads: int = 4
    top_k: int = 4


class Tiles(NamedTuple):
    norm_rows: int = 256
    mm_rows: int = 1024
    mm_cols: int = 1024
    mm_cols_res: int = 512
    attn_q: int = 512
    conv_rows: int = 256
    conv_halo: int = 32
    xattn_q: int = 512
    router_rows: int = 256
    expert_rows: int = 256
    expert_cols: int = 768
    down_cols: int = 512
    combine_rows: int = 128


def _params(*sem):
    return pltpu.CompilerParams(dimension_semantics=sem, vmem_limit_bytes=VMEM_LIMIT_BYTES)


def _rmsnorm_kernel(x_ref, g_ref, o_ref):
    x = x_ref[...]
    ms = jnp.mean(x * x, axis=-1, keepdims=True)
    o_ref[...] = (x * lax.rsqrt(ms + RMS_EPS) * g_ref[...]).astype(o_ref.dtype)


def _rmsnorm(x2d, g, rows):
    n, d = x2d.shape
    rows = min(rows, n)
    return pl.pallas_call(
        _rmsnorm_kernel,
        grid=(n // rows,),
        in_specs=[pl.BlockSpec((rows, d), lambda i: (i, 0)),
                  pl.BlockSpec((1, d), lambda i: (0, 0))],
        out_specs=pl.BlockSpec((rows, d), lambda i: (i, 0)),
        out_shape=jax.ShapeDtypeStruct((n, d), bf16),
        compiler_params=_params("parallel"),
        name="rmsnorm",
    )(x2d, g.reshape(1, d).astype(f32))


def _matmul_kernel(*refs, n_lhs, has_res):
    a_refs = refs[:n_lhs]
    w_refs = refs[n_lhs:2 * n_lhs]
    o_ref = refs[-1]
    acc = jnp.dot(a_refs[0][...], w_refs[0][...], preferred_element_type=f32)
    for p in range(1, n_lhs):
        acc = acc + jnp.dot(a_refs[p][...], w_refs[p][...], preferred_element_type=f32)
    if has_res:
        acc = acc + refs[2 * n_lhs][...]
    o_ref[...] = acc.astype(o_ref.dtype)


def _matmul(lhs_list, w, res, out_dtype, rows, cols, name):
    n_lhs = len(lhs_list)
    m, kp = lhs_list[0].shape
    nc = w.shape[1]
    assert w.shape[0] == n_lhs * kp
    rows = min(rows, m)
    cols = min(cols, nc)
    in_specs = [pl.BlockSpec((rows, kp), lambda i, j: (i, 0)) for _ in range(n_lhs)]
    in_specs += [pl.BlockSpec((kp, cols), lambda i, j, p=p: (p, j)) for p in range(n_lhs)]
    args = list(lhs_list) + [w] * n_lhs
    if res is not None:
        in_specs.append(pl.BlockSpec((rows, cols), lambda i, j: (i, j)))
        args.append(res)
    return pl.pallas_call(
        functools.partial(_matmul_kernel, n_lhs=n_lhs, has_res=res is not None),
        grid=(m // rows, nc // cols),
        in_specs=in_specs,
        out_specs=pl.BlockSpec((rows, cols), lambda i, j: (i, j)),
        out_shape=jax.ShapeDtypeStruct((m, nc), out_dtype),
        compiler_params=_params("parallel", "parallel"),
        name=name,
    )(*args)


def _diff_attn_kernel(slopes_ref, q_ref, k_ref, v_ref, lq1_ref, lk1_ref, lq2_ref, lk2_ref, g_ref,
                      o_ref, acc1_ref, acc2_ref, m1_ref, l1_ref, m2_ref, l2_ref, *, tq, d, lam_init):
    h = pl.program_id(1)
    qi = pl.program_id(2)
    slope = slopes_ref[h]
    scale = d ** -0.5
    rows = lax.broadcasted_iota(jnp.int32, (tq, tq), 0)
    cols = lax.broadcasted_iota(jnp.int32, (tq, tq), 1)
    dij = rows - cols
    bias = -slope * dij.astype(f32)
    q = q_ref[...]
    qs = (q[:, :d], q[:, d:])
    stats = ((m1_ref, l1_ref, acc1_ref), (m2_ref, l2_ref, acc2_ref))
    for m_ref, l_ref, acc_ref in stats:
        m_ref[...] = jnp.full(m_ref.shape, NEG_INF, f32)
        l_ref[...] = jnp.zeros(l_ref.shape, f32)
        acc_ref[...] = jnp.zeros(acc_ref.shape, f32)

    def block(j, masked):
        k0 = pl.multiple_of(j * tq, tq)
        kb = k_ref[pl.ds(k0, tq), :]
        vb = v_ref[pl.ds(k0, tq), :]
        cj = -slope * ((qi - j) * tq).astype(f32)
        for c, (m_ref, l_ref, acc_ref) in enumerate(stats):
            s = lax.dot_general(qs[c], kb[:, c * d:(c + 1) * d], (((1,), (1,)), ((), ())),
                                preferred_element_type=f32) * scale + bias
            if masked:
                s = jnp.where(dij >= 0, s, NEG_INF)
            m_old = m_ref[...]
            m_new = jnp.maximum(m_old, jnp.max(s, axis=-1, keepdims=True) + cj)
            p = jnp.exp(s - (m_new - cj))
            alpha = jnp.exp(m_old - m_new)
            l_ref[...] = alpha * l_ref[...] + jnp.sum(p, axis=-1, keepdims=True)
            acc_ref[...] = alpha * acc_ref[...] + jnp.dot(p.astype(bf16), vb,
                                                          preferred_element_type=f32)
            m_ref[...] = m_new

    def off_diag(j, carry):
        block(j, False)
        return carry

    lax.fori_loop(0, qi, off_diag, 0)
    block(qi, True)

    lam = (jnp.exp(jnp.sum(lq1_ref[...] * lk1_ref[...], axis=-1, keepdims=True))
           - jnp.exp(jnp.sum(lq2_ref[...] * lk2_ref[...], axis=-1, keepdims=True)) + lam_init)
    o = acc1_ref[...] / l1_ref[...] - lam * (acc2_ref[...] / l2_ref[...])
    ms = jnp.mean(o * o, axis=-1, keepdims=True)
    y = o * lax.rsqrt(ms + RMS_EPS) * g_ref[...] * (1.0 - lam_init)
    o_ref[...] = y.astype(o_ref.dtype)


def _diff_attention(proj3, lq1, lk1, lq2, lk2, g_subln, lam_init, cfg, tiles):
    b, s, _ = proj3.shape
    nh = cfg.diff_heads
    d = lq1.shape[-1]
    hw = 2 * d
    tq = min(tiles.attn_q, s)
    slopes = jnp.exp2(-8.0 * jnp.arange(1, nh + 1, dtype=f32) / nh)
    vec = lambda a: a.reshape(1, -1).astype(f32)
    small = lambda n: pl.BlockSpec((1, n), lambda bi, hi, qi: (0, 0))
    return pl.pallas_call(
        functools.partial(_diff_attn_kernel, tq=tq, d=d, lam_init=lam_init),
        grid=(b, nh, s // tq),
        in_specs=[pl.BlockSpec(memory_space=pltpu.SMEM),
                  pl.BlockSpec((None, tq, hw), lambda bi, hi, qi: (bi, qi, hi)),
                  pl.BlockSpec((None, s, hw), lambda bi, hi, qi: (bi, 0, nh + hi)),
                  pl.BlockSpec((None, s, hw), lambda bi, hi, qi: (bi, 0, 2 * nh + hi)),
                  small(d), small(d), small(d), small(d), small(hw)],
        out_specs=pl.BlockSpec((None, tq, hw), lambda bi, hi, qi: (bi, qi, hi)),
        out_shape=jax.ShapeDtypeStruct((b, s, nh * hw), bf16),
        scratch_shapes=[pltpu.VMEM((tq, hw), f32), pltpu.VMEM((tq, hw), f32),
                        pltpu.VMEM((tq, 1), f32), pltpu.VMEM((tq, 1), f32),
                        pltpu.VMEM((tq, 1), f32), pltpu.VMEM((tq, 1), f32)],
        compiler_params=_params("parallel", "parallel", "parallel"),
        name="diff_attention",
    )(slopes, proj3, proj3, proj3, vec(lq1), vec(lk1), vec(lq2), vec(lk2), vec(g_subln))


def _conformer_kernel(a_ref, gate_ref, ah_ref, gateh_ref, wdw_ref, bdw_ref, gln_ref, bln_ref,
                      wpw_ref, o_ref, buf_ref, conv_ref, *, ts, kw, halo):
    i = pl.program_id(1)
    ch = a_ref.shape[-1]
    glu = a_ref[...].astype(f32) * jax.nn.sigmoid(gate_ref[...].astype(f32))
    glu_h = ah_ref[...].astype(f32) * jax.nn.sigmoid(gateh_ref[...].astype(f32))
    buf_ref[0:halo, :] = jnp.where(i > 0, glu_h, 0.0)
    buf_ref[halo:halo + ts, :] = glu
    off = halo - (kw - 1)
    for c in range(ch // LANES):
        cs = slice(c * LANES, (c + 1) * LANES)
        acc = jnp.broadcast_to(bdw_ref[:, cs], (ts, LANES))
        for j in range(kw):
            acc = acc + wdw_ref[j:j + 1, cs] * buf_ref[off + j:off + j + ts, cs]
        conv_ref[:, cs] = acc
    c = conv_ref[...]
    mu = jnp.mean(c, axis=-1, keepdims=True)
    xc = c - mu
    var = jnp.mean(xc * xc, axis=-1, keepdims=True)
    y = xc * lax.rsqrt(var + LN_EPS) * gln_ref[...] + bln_ref[...]
    act = y * jax.nn.sigmoid(y)
    o_ref[...] = jnp.dot(act.astype(bf16), wpw_ref[...],
                         preferred_element_type=f32).astype(o_ref.dtype)


def _conformer(proj3, col0, w_dw, b_dw, g_ln, b_ln, w_pw, cfg, tiles):
    b, s, _ = proj3.shape
    kw = cfg.conv_width
    ch = w_pw.shape[0]
    ts = min(tiles.conv_rows, s)
    halo = tiles.conv_halo
    assert halo >= kw - 1 and ts % halo == 0 and col0 % ch == 0
    cb = col0 // ch
    hb = ts // halo
    vec = lambda a: a.reshape(1, ch).astype(f32)
    small = lambda: pl.BlockSpec((1, ch), lambda bi, i: (0, 0))
    return pl.pallas_call(
        functools.partial(_conformer_kernel, ts=ts, kw=kw, halo=halo),
        grid=(b, s // ts),
        in_specs=[pl.BlockSpec((None, ts, ch), lambda bi, i: (bi, i, cb)),
                  pl.BlockSpec((None, ts, ch), lambda bi, i: (bi, i, cb + 1)),
                  pl.BlockSpec((None, halo, ch), lambda bi, i: (bi, jnp.maximum(i * hb - 1, 0), cb)),
                  pl.BlockSpec((None, halo, ch), lambda bi, i: (bi, jnp.maximum(i * hb - 1, 0), cb + 1)),
                  pl.BlockSpec((kw, ch), lambda bi, i: (0, 0)),
                  small(), small(), small(),
                  pl.BlockSpec((ch, ch), lambda bi, i: (0, 0))],
        out_specs=pl.BlockSpec((None, ts, ch), lambda bi, i: (bi, i, 0)),
        out_shape=jax.ShapeDtypeStruct((b, s, ch), bf16),
        scratch_shapes=[pltpu.VMEM((halo + ts, ch), f32), pltpu.VMEM((ts, ch), f32)],
        compiler_params=_params("parallel", "parallel"),
        name="conformer",
    )(proj3, proj3, proj3, proj3, w_dw.reshape(kw, ch).astype(f32), vec(b_dw), vec(g_ln), vec(b_ln),
      w_pw)


def _xattn_kernel(q_ref, k_ref, v_ref, o_ref, *, scale):
    s = lax.dot_general(q_ref[...], k_ref[...], (((1,), (1,)), ((), ())),
                        preferred_element_type=f32) * scale
    m = jnp.max(s, axis=-1, keepdims=True)
    p = jnp.exp(s - m)
    p = p / jnp.sum(p, axis=-1, keepdims=True)
    o_ref[...] = jnp.dot(p.astype(bf16), v_ref[...], preferred_element_type=f32).astype(o_ref.dtype)


def _cross_attention(q3, k3, v3, cfg, tiles):
    b, s, dm = q3.shape
    m = k3.shape[1]
    nh = cfg.x_heads
    hd = dm // nh
    tq = min(tiles.xattn_q, s)
    return pl.pallas_call(
        functools.partial(_xattn_kernel, scale=hd ** -0.5),
        grid=(b, s // tq, nh),
        in_specs=[pl.BlockSpec((None, tq, hd), lambda bi, i, h: (bi, i, h)),
                  pl.BlockSpec((None, m, hd), lambda bi, i, h: (bi, 0, h)),
                  pl.BlockSpec((None, m, hd), lambda bi, i, h: (bi, 0, h))],
        out_specs=pl.BlockSpec((None, tq, hd), lambda bi, i, h: (bi, i, h)),
        out_shape=jax.ShapeDtypeStruct((b, s, dm), bf16),
        compiler_params=_params("parallel", "parallel", "parallel"),
        name="cross_attention",
    )(q3, k3, v3)


def _store_token_rows(o_ref, y, n_rows, n_sub):
    pitch = n_sub + SLAB_PAD
    for s in range(n_sub):
        o_ref[pl.ds(s, n_rows, stride=pitch), :] = y[:, s * LANES:(s + 1) * LANES]
    o_ref[pl.ds(n_sub, n_rows, stride=pitch), :] = jnp.zeros((n_rows, LANES), o_ref.dtype)


def _router_kernel(x_ref, g_ref, wr_ref, br_ref, hg_ref, te_ref, tw_ref, *, tm, n_sub, top_k):
    x = x_ref[...]
    ms = jnp.mean(x * x, axis=-1, keepdims=True)
    h = x * lax.rsqrt(ms + RMS_EPS) * g_ref[...]
    _store_token_rows(hg_ref, h, tm, n_sub)
    logits = jnp.dot(h.astype(bf16), wr_ref[...], preferred_element_type=f32) + br_ref[...]
    lane = lax.broadcasted_iota(jnp.int32, logits.shape, 1)
    lane_f = lane.astype(f32)
    vals, idxs = [], []
    for _ in range(top_k):
        m = jnp.max(logits, axis=-1, keepdims=True)
        idx = jnp.min(jnp.where(logits == m, lane_f, float(LANES)), axis=-1, keepdims=True)
        vals.append(m)
        idxs.append(idx)
        logits = jnp.where(lane_f == idx, NEG_INF * 2.0, logits)
    exps = [jnp.exp(v - vals[0]) for v in vals]
    denom = exps[0]
    for e in exps[1:]:
        denom = denom + e
    te = jnp.zeros(logits.shape, f32)
    tw = jnp.zeros(logits.shape, f32)
    for k in range(top_k):
        te = jnp.where(lane == k, idxs[k], te)
        tw = jnp.where(lane == k, exps[k] / denom, tw)
    te_ref[...] = te.astype(jnp.int32)
    tw_ref[...] = tw


def _norm_router(x2d, g, w_router, b_router, cfg, tiles):
    n, d = x2d.shape
    e = w_router.shape[1]
    assert e <= LANES and d % LANES == 0
    n_sub = d // LANES
    pitch = n_sub + SLAB_PAD
    tm = min(tiles.router_rows, n)
    wr = jnp.zeros((d, LANES), bf16).at[:, :e].set(w_router.astype(bf16))
    br = jnp.full((1, LANES), NEG_INF, f32).at[0, :e].set(b_router.astype(f32))
    hg, te, tw = pl.pallas_call(
        functools.partial(_router_kernel, tm=tm, n_sub=n_sub, top_k=cfg.top_k),
        grid=(n // tm,),
        in_specs=[pl.BlockSpec((tm, d), lambda i: (i, 0)),
                  pl.BlockSpec((1, d), lambda i: (0, 0)),
                  pl.BlockSpec((d, LANES), lambda i: (0, 0)),
                  pl.BlockSpec((1, LANES), lambda i: (0, 0))],
        out_specs=[pl.BlockSpec((tm * pitch, LANES), lambda i: (i, 0)),
                   pl.BlockSpec((tm, LANES), lambda i: (i, 0)),
                   pl.BlockSpec((tm, LANES), lambda i: (i, 0))],
        out_shape=[jax.ShapeDtypeStruct((n * pitch, LANES), f32),
                   jax.ShapeDtypeStruct((n, LANES), jnp.int32),
                   jax.ShapeDtypeStruct((n, LANES), f32)],
        compiler_params=_params("parallel"),
        name="norm_router",
    )(x2d, g.reshape(1, d).astype(f32), wr, br)
    return hg, te[:, :cfg.top_k], tw[:, :cfg.top_k]


def _routing_plan(top_e, gate_w, n_experts, blk):
    i32 = jnp.int32
    n_tok, top_k = top_e.shape
    n_slot = n_tok * top_k
    slot_e = top_e.reshape(n_slot)
    slot_w = gate_w.reshape(n_slot)
    order = jnp.argsort(slot_e, stable=True).astype(i32)
    inv = jnp.argsort(order).astype(i32)
    experts = jnp.arange(n_experts, dtype=i32)
    counts = jnp.sum((slot_e[:, None] == experts[None, :]).astype(i32), axis=0)
    padded = (counts + blk - 1) // blk * blk
    pad_end = jnp.cumsum(padded)
    pad_start = pad_end - padded
    start = jnp.cumsum(counts) - counts
    shift = pad_start - start
    n_blocks = -(-n_slot // blk) + n_experts
    n_rows = n_blocks * blk
    n_used = pad_end[-1] // blk
    last_e = jnp.max(jnp.where(counts > 0, experts, 0))
    block_start = jnp.arange(n_blocks, dtype=i32) * blk
    block_e = jnp.minimum(jnp.sum((pad_end[None, :] <= block_start[:, None]).astype(i32), axis=1), last_e)
    row = jnp.arange(n_rows, dtype=i32)
    row_e = jnp.repeat(block_e, blk)
    valid = (row - pad_start[row_e] < counts[row_e]) & (row < pad_end[-1])
    row_slot = order[jnp.where(valid, row - shift[row_e], 0)]
    row_tok = jnp.where(valid, row_slot // top_k, 0)
    row_w = jnp.where(valid, slot_w[row_slot], 0.0)
    pos = inv + shift[slot_e]
    return row_tok, row_w.reshape(n_rows, 1), block_e, n_used.reshape(1).astype(i32), pos


def _gate_up_kernel(be_ref, tok_ref, nu_ref, hg_hbm, wg_ref, bg_ref, wu_ref, bu_ref, o_ref,
                    x3a_ref, x3b_ref, x_ref, sem, *, blk, n_sub, nf):
    f = pl.program_id(0)
    b = pl.program_id(1)
    n_used = nu_ref[0]
    pitch = n_sub + SLAB_PAD
    bufs = (x3a_ref, x3b_ref)
    u = f * n_used + b

    def issue(block_idx, slot):
        def body(r, carry):
            tok = tok_ref[block_idx * blk + r]
            pltpu.make_async_copy(hg_hbm.at[pl.ds(tok * pitch, n_sub)], bufs[slot].at[:, r, :],
                                  sem.at[slot]).start()
            return carry
        lax.fori_loop(0, blk, body, 0)

    def wait_all(slot):
        pltpu.make_async_copy(bufs[slot], bufs[slot], sem.at[slot]).wait()

    @pl.when(b < n_used)
    def _():
        @pl.when(u == 0)
        def _():
            issue(0, 0)
        for slot in range(2):
            @pl.when(u % 2 == slot)
            def _():
                @pl.when((b + 1 < n_used) | (f + 1 < nf))
                def _():
                    issue(jnp.where(b + 1 < n_used, b + 1, 0), 1 - slot)
                wait_all(slot)
                for s in range(n_sub):
                    x_ref[:, s * LANES:(s + 1) * LANES] = bufs[slot][s].astype(bf16)
        x = x_ref[...]
        g = jnp.minimum(jnp.dot(x, wg_ref[...], preferred_element_type=f32) + bg_ref[...], SWIGLU_LIMIT)
        up = jnp.clip(jnp.dot(x, wu_ref[...], preferred_element_type=f32) + bu_ref[...],
                      -SWIGLU_LIMIT, SWIGLU_LIMIT)
        o_ref[...] = ((up + 1.0) * (g * jax.nn.sigmoid(SWIGLU_ALPHA * g))).astype(o_ref.dtype)

    @pl.when(b >= n_used)
    def _():
        o_ref[...] = jnp.zeros(o_ref.shape, o_ref.dtype)


def _down_kernel(be_ref, nu_ref, a_ref, wd_ref, bd_ref, rw_ref, o_ref, y_ref, *, blk, n_sub, tc):
    b = pl.program_id(0)
    d = n_sub * LANES

    @pl.when(b < nu_ref[0])
    def _():
        a = a_ref[...]
        rw = rw_ref[...]
        for c in range(d // tc):
            cs = slice(c * tc, (c + 1) * tc)
            y = jnp.dot(a, wd_ref[:, cs], preferred_element_type=f32) + bd_ref[:, cs]
            y_ref[:, cs] = y * rw
        _store_token_rows(o_ref, y_ref, blk, n_sub)

    @pl.when(b >= nu_ref[0])
    def _():
        o_ref[...] = jnp.zeros(o_ref.shape, o_ref.dtype)


def _experts(hg, row_tok, row_w, block_e, n_used, wg, bg, wu, bu, wd, bd, tiles):
    n_e, d, ff = wg.shape
    n_sub = d // LANES
    pitch = n_sub + SLAB_PAD
    blk = tiles.expert_rows
    n_rows = row_tok.shape[0]
    n_blocks = n_rows // blk
    tf = min(tiles.expert_cols, ff)
    nf = ff // tf
    act = pl.pallas_call(
        functools.partial(_gate_up_kernel, blk=blk, n_sub=n_sub, nf=nf),
        grid_spec=pltpu.PrefetchScalarGridSpec(
            num_scalar_prefetch=3,
            grid=(nf, n_blocks),
            in_specs=[pl.BlockSpec(memory_space=pl.ANY),
                      pl.BlockSpec((None, d, tf), lambda f, b, be, tok, nu: (be[b], 0, f)),
                      pl.BlockSpec((None, 1, tf), lambda f, b, be, tok, nu: (be[b], 0, f)),
                      pl.BlockSpec((None, d, tf), lambda f, b, be, tok, nu: (be[b], 0, f)),
                      pl.BlockSpec((None, 1, tf), lambda f, b, be, tok, nu: (be[b], 0, f))],
            out_specs=pl.BlockSpec((blk, tf), lambda f, b, be, tok, nu: (b, f)),
            scratch_shapes=[pltpu.VMEM((n_sub, blk, LANES), f32), pltpu.VMEM((n_sub, blk, LANES), f32),
                            pltpu.VMEM((blk, d), bf16), pltpu.SemaphoreType.DMA((2,))]),
        out_shape=jax.ShapeDtypeStruct((n_rows, ff), bf16),
        compiler_params=_params("arbitrary", "arbitrary"),
        name="expert_gate_up",
    )(block_e, row_tok, n_used, hg, wg, bg.reshape(n_e, 1, ff).astype(f32), wu,
      bu.reshape(n_e, 1, ff).astype(f32))
    tc = min(tiles.down_cols, d)
    return pl.pallas_call(
        functools.partial(_down_kernel, blk=blk, n_sub=n_sub, tc=tc),
        grid_spec=pltpu.PrefetchScalarGridSpec(
            num_scalar_prefetch=2,
            grid=(n_blocks,),
            in_specs=[pl.BlockSpec((blk, ff), lambda b, be, nu: (b, 0)),
                      pl.BlockSpec((None, ff, d), lambda b, be, nu: (be[b], 0, 0)),
                      pl.BlockSpec((None, 1, d), lambda b, be, nu: (be[b], 0, 0)),
                      pl.BlockSpec((blk, 1), lambda b, be, nu: (b, 0))],
            out_specs=pl.BlockSpec((blk * pitch, LANES), lambda b, be, nu: (b, 0)),
            scratch_shapes=[pltpu.VMEM((blk, d), f32)]),
        out_shape=jax.ShapeDtypeStruct((n_rows * pitch, LANES), f32),
        compiler_params=_params("arbitrary"),
        name="expert_down",
    )(block_e, n_used, act, wd, bd.reshape(n_e, 1, d).astype(f32), row_w)


def _combine_kernel(pos_ref, x_ref, ys_hbm, g_ref, o_ref, b0_ref, b1_ref, sem, *, tm, n_sub, top_k,
                    final_norm):
    i = pl.program_id(0)
    nb = pl.num_programs(0)
    bufs = (b0_ref, b1_ref)
    pitch = n_sub + SLAB_PAD

    def issue(block_idx, slot):
        def body(t, carry):
            for k in range(top_k):
                row = pos_ref[(block_idx * tm + t) * top_k + k]
                pltpu.make_async_copy(ys_hbm.at[pl.ds(row * pitch, n_sub)],
                                      bufs[slot].at[:, k * tm + t, :], sem.at[slot]).start()
            return carry
        lax.fori_loop(0, tm, body, 0)

    def wait_all(slot):
        pltpu.make_async_copy(bufs[slot], bufs[slot], sem.at[slot]).wait()

    @pl.when(i == 0)
    def _():
        issue(0, 0)

    for slot in range(2):
        @pl.when(i % 2 == slot)
        def _():
            @pl.when(i + 1 < nb)
            def _():
                issue(i + 1, 1 - slot)
            wait_all(slot)
            for s in range(n_sub):
                cs = slice(s * LANES, (s + 1) * LANES)
                acc = x_ref[:, cs]
                for k in range(top_k):
                    acc = acc + bufs[slot][s, k * tm:(k + 1) * tm, :]
                o_ref[:, cs] = acc

    if final_norm:
        x = o_ref[...]
        ms = jnp.mean(x * x, axis=-1, keepdims=True)
        o_ref[...] = x * lax.rsqrt(ms + RMS_EPS) * g_ref[...]


def _combine(x2d, ys, pos, g_final, final_norm, cfg, tiles):
    n, d = x2d.shape
    n_sub = d // LANES
    tm = min(tiles.combine_rows, n)
    top_k = cfg.top_k
    grid_spec = pltpu.PrefetchScalarGridSpec(
        num_scalar_prefetch=1,
        grid=(n // tm,),
        in_specs=[pl.BlockSpec((tm, d), lambda i, pos: (i, 0)),
                  pl.BlockSpec(memory_space=pl.ANY),
                  pl.BlockSpec((1, d), lambda i, pos: (0, 0))],
        out_specs=pl.BlockSpec((tm, d), lambda i, pos: (i, 0)),
        scratch_shapes=[pltpu.VMEM((n_sub, top_k * tm, LANES), f32),
                        pltpu.VMEM((n_sub, top_k * tm, LANES), f32),
                        pltpu.SemaphoreType.DMA((2,))],
    )
    return pl.pallas_call(
        functools.partial(_combine_kernel, tm=tm, n_sub=n_sub, top_k=top_k, final_norm=final_norm),
        grid_spec=grid_spec,
        out_shape=jax.ShapeDtypeStruct((n, d), f32),
        compiler_params=_params("arbitrary"),
        name="moe_combine",
    )(pos, x2d, ys, g_final.reshape(1, d).astype(f32))


def _layer(x, mem, p, l, is_last, g_final, cfg, tiles):
    b, s, d = x.shape
    n = b * s
    m_len = mem.shape[1]
    diff_width = d // 2
    conv_ch = d - diff_width
    x2d = x.reshape(n, d)

    h = _rmsnorm(x2d, p['g_mix'][l], tiles.norm_rows)
    proj = _matmul([h], p['w_in'][l].astype(bf16), None, bf16, tiles.mm_rows, tiles.mm_cols, "in_proj")
    proj3 = proj.reshape(b, s, -1)
    lam_init = 0.8 - 0.6 * math.exp(-0.3 * l)
    a_out = _diff_attention(proj3, p['lambda_q1'][l], p['lambda_k1'][l], p['lambda_q2'][l],
                            p['lambda_k2'][l], p['g_subln'][l], lam_init, cfg, tiles)
    c_out = _conformer(proj3, 3 * diff_width, p['w_dw'][l], p['b_dw'][l], p['g_conv_ln'][l],
                       p['b_conv_ln'][l], p['w_conv_pw'][l].astype(bf16), cfg, tiles)
    assert diff_width == conv_ch
    x2d = _matmul([a_out.reshape(n, diff_width), c_out.reshape(n, conv_ch)],
                  p['w_out'][l].astype(bf16), x2d, f32, tiles.mm_rows, tiles.mm_cols_res, "out_proj")

    hc = _rmsnorm(x2d, p['g_cross'][l], tiles.norm_rows)
    mem_n = _rmsnorm(mem.reshape(b * m_len, d), p['g_mem'][l], tiles.norm_rows)
    qc = _matmul([hc], p['w_cq'][l].astype(bf16), None, bf16, tiles.mm_rows, tiles.mm_cols, "xq_proj")
    kc = _matmul([mem_n], p['w_ck'][l].astype(bf16), None, bf16, tiles.mm_rows, tiles.mm_cols, "xk_proj")
    vc = _matmul([mem_n], p['w_cv'][l].astype(bf16), None, bf16, tiles.mm_rows, tiles.mm_cols, "xv_proj")
    oc = _cross_attention(qc.reshape(b, s, d), kc.reshape(b, m_len, d), vc.reshape(b, m_len, d),
                          cfg, tiles)
    x2d = _matmul([oc.reshape(n, d)], p['w_co'][l].astype(bf16), x2d, f32, tiles.mm_rows,
                  tiles.mm_cols_res, "xo_proj")

    n_experts = p['w_router'].shape[-1]
    hg, top_e, gate_w = _norm_router(x2d, p['g_ffn'][l], p['w_router'][l], p['b_router'][l], cfg, tiles)
    row_tok, row_w, block_e, n_used, pos = _routing_plan(top_e, gate_w, n_experts, tiles.expert_rows)
    ys = _experts(hg, row_tok, row_w, block_e, n_used, p['w_gate'][l].astype(bf16), p['b_gate'][l],
                  p['w_up'][l].astype(bf16), p['b_up'][l], p['w_down'][l].astype(bf16), p['b_down'][l],
                  tiles)
    x2d = _combine(x2d, ys, pos, g_final, is_last, cfg, tiles)
    return x2d.reshape(b, s, d)


def _forward(x, mem, params, g_final, cfg, tiles):
    depth = params['g_mix'].shape[0]
    for l in range(depth):
        x = _layer(x, mem, params, l, l == depth - 1, g_final, cfg, tiles)
    return x


def kernel(x, mem, g_mix, w_in, lambda_q1, lambda_k1, lambda_q2, lambda_k2, g_subln, w_dw, b_dw,
           g_conv_ln, b_conv_ln, w_conv_pw, w_out, g_cross, g_mem, w_cq, w_ck, w_cv, w_co, g_ffn,
           w_router, b_router, w_gate, b_gate, w_up, b_up, w_down, b_down, g_final):
    params = dict(g_mix=g_mix, w_in=w_in, lambda_q1=lambda_q1, lambda_k1=lambda_k1,
                  lambda_q2=lambda_q2, lambda_k2=lambda_k2, g_subln=g_subln, w_dw=w_dw, b_dw=b_dw,
                  g_conv_ln=g_conv_ln, b_conv_ln=b_conv_ln, w_conv_pw=w_conv_pw, w_out=w_out,
                  g_cross=g_cross, g_mem=g_mem, w_cq=w_cq, w_ck=w_ck, w_cv=w_cv, w_co=w_co,
                  g_ffn=g_ffn, w_router=w_router, b_router=b_router, w_gate=w_gate, b_gate=b_gate,
                  w_up=w_up, b_up=b_up, w_down=w_down, b_down=b_down)
    return _forward(x, mem, params, g_final, Cfg(), Tiles())
```

```python
import functools
import math
from typing import NamedTuple

import jax
import jax.numpy as jnp
from jax import lax
from jax.experimental import pallas as pl
from jax.experimental.pallas import tpu as pltpu

f32 = jnp.float32
bf16 = jnp.bfloat16

RMS_EPS = 1e-6
LN_EPS = 1e-5
NEG_INF = -1e30
SWIGLU_LIMIT = 7.0
SWIGLU_ALPHA = 1.702
LOG2E = 1.4426950408889634

LANES = 128
SUBLANES = 8
SLAB_PAD = 1
VMEM_LIMIT_BYTES = 56 * 1024 * 1024


class Cfg(NamedTuple):
    diff_heads: int = 8
    conv_width: int = 31
    x_heads: int = 4
    top_k: int = 4


class Tiles(NamedTuple):
    norm_rows: int = 256
    mm_rows: int = 1024
    mm_cols: int = 1024
    mm_cols_res: int = 512
    attn_q: int = 1024
    conv_rows: int = 256
    conv_halo: int = 32
    xattn_q: int = 512
    router_rows: int = 256
    expert_rows: int = 256
    gather_unroll: int = 8
    expert_cols: int = 768
    down_cols: int = 512
    combine_rows: int = 128


def _params(*sem):
    return pltpu.CompilerParams(dimension_semantics=sem, vmem_limit_bytes=VMEM_LIMIT_BYTES)


def _rmsnorm_kernel(x_ref, g_ref, o_ref):
    x = x_ref[...]
    ms = jnp.mean(x * x, axis=-1, keepdims=True)
    o_ref[...] = (x * lax.rsqrt(ms + RMS_EPS) * g_ref[...]).astype(o_ref.dtype)


def _rmsnorm(x2d, g, rows):
    n, d = x2d.shape
    rows = min(rows, n)
    return pl.pallas_call(
        _rmsnorm_kernel,
        grid=(n // rows,),
        in_specs=[pl.BlockSpec((rows, d), lambda i: (i, 0)),
                  pl.BlockSpec((1, d), lambda i: (0, 0))],
        out_specs=pl.BlockSpec((rows, d), lambda i: (i, 0)),
        out_shape=jax.ShapeDtypeStruct((n, d), bf16),
        compiler_params=_params("parallel"),
        name="rmsnorm",
    )(x2d, g.reshape(1, d).astype(f32))


def _matmul_kernel(*refs, n_lhs, has_res, has_scale):
    a_refs = refs[:n_lhs]
    w_refs = refs[n_lhs:2 * n_lhs]
    extra = list(refs[2 * n_lhs:-1])
    o_ref = refs[-1]
    acc = jnp.dot(a_refs[0][...], w_refs[0][...], preferred_element_type=f32)
    for p in range(1, n_lhs):
        acc = acc + jnp.dot(a_refs[p][...], w_refs[p][...], preferred_element_type=f32)
    if has_res:
        acc = acc + extra.pop(0)[...]
    if has_scale:
        acc = acc * extra.pop(0)[...]
    o_ref[...] = acc.astype(o_ref.dtype)


def _matmul(lhs_list, w, res, out_dtype, rows, cols, name, col_scale=None):
    n_lhs = len(lhs_list)
    m, kp = lhs_list[0].shape
    nc = w.shape[1]
    assert w.shape[0] == n_lhs * kp
    rows = min(rows, m)
    cols = min(cols, nc)
    in_specs = [pl.BlockSpec((rows, kp), lambda i, j: (i, 0)) for _ in range(n_lhs)]
    in_specs += [pl.BlockSpec((kp, cols), lambda i, j, p=p: (p, j)) for p in range(n_lhs)]
    args = list(lhs_list) + [w] * n_lhs
    if res is not None:
        in_specs.append(pl.BlockSpec((rows, cols), lambda i, j: (i, j)))
        args.append(res)
    if col_scale is not None:
        in_specs.append(pl.BlockSpec((1, cols), lambda i, j: (0, j)))
        args.append(col_scale.reshape(1, nc).astype(f32))
    return pl.pallas_call(
        functools.partial(_matmul_kernel, n_lhs=n_lhs, has_res=res is not None,
                          has_scale=col_scale is not None),
        grid=(m // rows, nc // cols),
        in_specs=in_specs,
        out_specs=pl.BlockSpec((rows, cols), lambda i, j: (i, j)),
        out_shape=jax.ShapeDtypeStruct((m, nc), out_dtype),
        compiler_params=_params("parallel", "parallel"),
        name=name,
    )(*args)


def _bf16_part(x):
    bits = lax.bitcast_convert_type(x, jnp.uint32) & jnp.uint32(0xFFFF0000)
    return lax.bitcast_convert_type(bits, f32)


def _lane_tile(x, n):
    return x if n == 1 else jnp.concatenate([x] * n, axis=1)


def _diff_attn_kernel(slopes_ref, q_ref, k_ref, v_ref, lq1_ref, lk1_ref, lq2_ref, lk2_ref, g_ref,
                      o_ref, qa1_ref, qa2_ref, ka1_ref, ka2_ref, acc1_ref, acc2_ref,
                      m1_ref, l1_ref, m2_ref, l2_ref, *, tq, d, lam_init):
    h = pl.program_id(1)
    qi = pl.program_id(2)
    tk = tq
    slope2 = slopes_ref[h]
    lane_q = lax.broadcasted_iota(jnp.int32, (tq, d), 1)
    ones_cols = jnp.where(lane_q < 3, 1.0, 0.0).astype(bf16)
    q = q_ref[...]
    qa1_ref[:, :d] = q[:, :d]
    qa1_ref[:, d:] = ones_cols
    qa2_ref[:, :d] = q[:, d:]
    qa2_ref[:, d:] = ones_cols
    lane_k = lax.broadcasted_iota(jnp.int32, (tk, d), 1)
    v0 = slope2 * lax.broadcasted_iota(jnp.int32, (tk, d), 0).astype(f32)
    hi = _bf16_part(v0)
    r1 = v0 - hi
    lo = _bf16_part(r1)
    lo2 = r1 - lo
    bias_cols = jnp.where(lane_k == 0, hi, jnp.where(lane_k == 1, lo, jnp.where(lane_k == 2, lo2, 0.0)))
    ka1_ref[:, d:] = bias_cols.astype(bf16)
    ka2_ref[:, d:] = bias_cols.astype(bf16)
    rel = (lax.broadcasted_iota(jnp.int32, (tq, tk), 0)
           - lax.broadcasted_iota(jnp.int32, (tq, tk), 1))
    maps = ((qa1_ref, ka1_ref, m1_ref, l1_ref, acc1_ref), (qa2_ref, ka2_ref, m2_ref, l2_ref, acc2_ref))
    for _, _, m_ref, l_ref, acc_ref in maps:
        m_ref[...] = jnp.full(m_ref.shape, NEG_INF, f32)
        l_ref[...] = jnp.zeros(l_ref.shape, f32)
        acc_ref[...] = jnp.zeros(acc_ref.shape, f32)

    def block(j, masked):
        k0 = pl.multiple_of(j * tk, tk)
        kb = k_ref[pl.ds(k0, tk), :]
        vb = v_ref[pl.ds(k0, tk), :]
        cj = slope2 * ((j - qi) * tk).astype(f32)
        for c, (qa_ref, ka_ref, m_ref, l_ref, acc_ref) in enumerate(maps):
            ka_ref[:, :d] = kb[:, c * d:(c + 1) * d]
            s = lax.dot_general(qa_ref[...], ka_ref[...], (((1,), (1,)), ((), ())),
                                preferred_element_type=f32)
            if masked:
                s = jnp.where(rel >= 0, s, NEG_INF)
            m_old = m_ref[...]
            m_new = jnp.maximum(m_old, jnp.max(s, axis=-1, keepdims=True) + cj)
            p = jnp.exp2(s - _lane_tile(m_new - cj, tk // LANES))
            alpha = jnp.exp2(m_old - m_new)
            l_ref[...] = alpha * l_ref[...] + jnp.sum(p, axis=-1, keepdims=True)
            acc_ref[...] = (_lane_tile(alpha, acc_ref.shape[1] // LANES) * acc_ref[...]
                            + jnp.dot(p.astype(bf16), vb, preferred_element_type=f32))
            m_ref[...] = m_new

    def off_diag(j, carry):
        block(j, False)
        return carry

    lax.fori_loop(0, qi, off_diag, 0)
    block(qi, True)

    lam = (jnp.exp(jnp.sum(lq1_ref[...] * lk1_ref[...], axis=-1, keepdims=True))
           - jnp.exp(jnp.sum(lq2_ref[...] * lk2_ref[...], axis=-1, keepdims=True)) + lam_init)
    reps = acc1_ref.shape[1] // LANES
    o = (acc1_ref[...] / _lane_tile(l1_ref[...], reps)
         - lam * (acc2_ref[...] / _lane_tile(l2_ref[...], reps)))
    ms = jnp.mean(o * o, axis=-1, keepdims=True)
    y = o * lax.rsqrt(ms + RMS_EPS) * g_ref[...] * (1.0 - lam_init)
    o_ref[...] = y.astype(o_ref.dtype)


def _diff_attention(proj3, lq1, lk1, lq2, lk2, g_subln, lam_init, cfg, tiles):
    b, s, _ = proj3.shape
    nh = cfg.diff_heads
    d = lq1.shape[-1]
    hw = 2 * d
    tq = min(tiles.attn_q, s)
    slopes = jnp.exp2(-8.0 * jnp.arange(1, nh + 1, dtype=f32) / nh) * LOG2E
    vec = lambda a: a.reshape(1, -1).astype(f32)
    small = lambda n: pl.BlockSpec((1, n), lambda bi, hi, qi: (0, 0))
    return pl.pallas_call(
        functools.partial(_diff_attn_kernel, tq=tq, d=d, lam_init=lam_init),
        grid=(b, nh, s // tq),
        in_specs=[pl.BlockSpec(memory_space=pltpu.SMEM),
                  pl.BlockSpec((None, tq, hw), lambda bi, hi, qi: (bi, qi, hi)),
                  pl.BlockSpec((None, s, hw), lambda bi, hi, qi: (bi, 0, nh + hi)),
                  pl.BlockSpec((None, s, hw), lambda bi, hi, qi: (bi, 0, 2 * nh + hi)),
                  small(d), small(d), small(d), small(d), small(hw)],
        out_specs=pl.BlockSpec((None, tq, hw), lambda bi, hi, qi: (bi, qi, hi)),
        out_shape=jax.ShapeDtypeStruct((b, s, nh * hw), bf16),
        scratch_shapes=[pltpu.VMEM((tq, hw), bf16), pltpu.VMEM((tq, hw), bf16),
                        pltpu.VMEM((tq, hw), bf16), pltpu.VMEM((tq, hw), bf16),
                        pltpu.VMEM((tq, hw), f32), pltpu.VMEM((tq, hw), f32),
                        pltpu.VMEM((tq, LANES), f32), pltpu.VMEM((tq, LANES), f32),
                        pltpu.VMEM((tq, LANES), f32), pltpu.VMEM((tq, LANES), f32)],
        compiler_params=_params("parallel", "parallel", "parallel"),
        name="diff_attention",
    )(slopes, proj3, proj3, proj3, vec(lq1), vec(lk1), vec(lq2), vec(lk2), vec(g_subln))


def _conformer_kernel(a_ref, gate_ref, ah_ref, gateh_ref, wdw_ref, bdw_ref, gln_ref, bln_ref,
                      wpw_ref, o_ref, buf_ref, conv_ref, *, ts, kw, halo):
    i = pl.program_id(1)
    ch = a_ref.shape[-1]
    glu = a_ref[...].astype(f32) * jax.nn.sigmoid(gate_ref[...].astype(f32))
    glu_h = ah_ref[...].astype(f32) * jax.nn.sigmoid(gateh_ref[...].astype(f32))
    buf_ref[0:halo, :] = jnp.where(i > 0, glu_h, 0.0)
    buf_ref[halo:halo + ts, :] = glu
    off = halo - (kw - 1)
    for c in range(ch // LANES):
        cs = slice(c * LANES, (c + 1) * LANES)
        acc = jnp.broadcast_to(bdw_ref[:, cs], (ts, LANES))
        for j in range(kw):
            acc = acc + wdw_ref[j:j + 1, cs] * buf_ref[off + j:off + j + ts, cs]
        conv_ref[:, cs] = acc
    c = conv_ref[...]
    mu = jnp.mean(c, axis=-1, keepdims=True)
    xc = c - mu
    var = jnp.mean(xc * xc, axis=-1, keepdims=True)
    y = xc * lax.rsqrt(var + LN_EPS) * gln_ref[...] + bln_ref[...]
    act = y * jax.nn.sigmoid(y)
    o_ref[...] = jnp.dot(act.astype(bf16), wpw_ref[...],
                         preferred_element_type=f32).astype(o_ref.dtype)


def _conformer(proj3, col0, w_dw, b_dw, g_ln, b_ln, w_pw, cfg, tiles):
    b, s, _ = proj3.shape
    kw = cfg.conv_width
    ch = w_pw.shape[0]
    ts = min(tiles.conv_rows, s)
    halo = tiles.conv_halo
    assert halo >= kw - 1 and ts % halo == 0 and col0 % ch == 0
    cb = col0 // ch
    hb = ts // halo
    vec = lambda a: a.reshape(1, ch).astype(f32)
    small = lambda: pl.BlockSpec((1, ch), lambda bi, i: (0, 0))
    return pl.pallas_call(
        functools.partial(_conformer_kernel, ts=ts, kw=kw, halo=halo),
        grid=(b, s // ts),
        in_specs=[pl.BlockSpec((None, ts, ch), lambda bi, i: (bi, i, cb)),
                  pl.BlockSpec((None, ts, ch), lambda bi, i: (bi, i, cb + 1)),
                  pl.BlockSpec((None, halo, ch), lambda bi, i: (bi, jnp.maximum(i * hb - 1, 0), cb)),
                  pl.BlockSpec((None, halo, ch), lambda bi, i: (bi, jnp.maximum(i * hb - 1, 0), cb + 1)),
                  pl.BlockSpec((kw, ch), lambda bi, i: (0, 0)),
                  small(), small(), small(),
                  pl.BlockSpec((ch, ch), lambda bi, i: (0, 0))],
        out_specs=pl.BlockSpec((None, ts, ch), lambda bi, i: (bi, i, 0)),
        out_shape=jax.ShapeDtypeStruct((b, s, ch), bf16),
        scratch_shapes=[pltpu.VMEM((halo + ts, ch), f32), pltpu.VMEM((ts, ch), f32)],
        compiler_params=_params("parallel", "parallel"),
        name="conformer",
    )(proj3, proj3, proj3, proj3, w_dw.reshape(kw, ch).astype(f32), vec(b_dw), vec(g_ln), vec(b_ln),
      w_pw)


def _xattn_kernel(q_ref, k_ref, v_ref, o_ref, *, scale):
    s = lax.dot_general(q_ref[...], k_ref[...], (((1,), (1,)), ((), ())),
                        preferred_element_type=f32) * scale
    m = jnp.max(s, axis=-1, keepdims=True)
    p = jnp.exp(s - m)
    p = p / jnp.sum(p, axis=-1, keepdims=True)
    o_ref[...] = jnp.dot(p.astype(bf16), v_ref[...], preferred_element_type=f32).astype(o_ref.dtype)


def _cross_attention(q3, k3, v3, cfg, tiles):
    b, s, dm = q3.shape
    m = k3.shape[1]
    nh = cfg.x_heads
    hd = dm // nh
    tq = min(tiles.xattn_q, s)
    return pl.pallas_call(
        functools.partial(_xattn_kernel, scale=hd ** -0.5),
        grid=(b, s // tq, nh),
        in_specs=[pl.BlockSpec((None, tq, hd), lambda bi, i, h: (bi, i, h)),
                  pl.BlockSpec((None, m, hd), lambda bi, i, h: (bi, 0, h)),
                  pl.BlockSpec((None, m, hd), lambda bi, i, h: (bi, 0, h))],
        out_specs=pl.BlockSpec((None, tq, hd), lambda bi, i, h: (bi, i, h)),
        out_shape=jax.ShapeDtypeStruct((b, s, dm), bf16),
        compiler_params=_params("parallel", "parallel", "parallel"),
        name="cross_attention",
    )(q3, k3, v3)


def _store_token_rows(o_ref, y, n_rows, n_sub):
    pitch = n_sub + SLAB_PAD
    for s in range(n_sub):
        o_ref[pl.ds(s, n_rows, stride=pitch), :] = y[:, s * LANES:(s + 1) * LANES]
    o_ref[pl.ds(n_sub, n_rows, stride=pitch), :] = jnp.zeros((n_rows, LANES), o_ref.dtype)


def _router_kernel(x_ref, g_ref, wr_ref, br_ref, hg_ref, te_ref, tw_ref, cnt_ref, *, tm, n_sub, top_k):
    x = x_ref[...]
    ms = jnp.mean(x * x, axis=-1, keepdims=True)
    h = x * lax.rsqrt(ms + RMS_EPS) * g_ref[...]
    _store_token_rows(hg_ref, h, tm, n_sub)
    logits = jnp.dot(h.astype(bf16), wr_ref[...], preferred_element_type=f32) + br_ref[...]
    lane = lax.broadcasted_iota(jnp.int32, logits.shape, 1)
    lane_f = lane.astype(f32)
    vals, idxs = [], []
    for _ in range(top_k):
        m = jnp.max(logits, axis=-1, keepdims=True)
        idx = jnp.min(jnp.where(logits == m, lane_f, float(LANES)), axis=-1, keepdims=True)
        vals.append(m)
        idxs.append(idx)
        logits = jnp.where(lane_f == idx, NEG_INF * 2.0, logits)
    exps = [jnp.exp(v - vals[0]) for v in vals]
    denom = exps[0]
    for e in exps[1:]:
        denom = denom + e
    te = jnp.zeros(logits.shape, f32)
    tw = jnp.zeros(logits.shape, f32)
    hits = jnp.zeros(logits.shape, f32)
    for k in range(top_k):
        te = jnp.where(lane == k, idxs[k], te)
        tw = jnp.where(lane == k, exps[k] / denom, tw)
        hits = hits + jnp.where(lane_f == idxs[k], 1.0, 0.0)
    te_ref[...] = te.astype(jnp.int32)
    tw_ref[...] = tw
    tile_counts = jnp.sum(hits, axis=0, keepdims=True)
    row8 = lax.broadcasted_iota(jnp.int32, cnt_ref.shape, 0)
    cnt_ref[...] = jnp.where(row8 == 0, tile_counts, 0.0).astype(jnp.int32)


def _norm_router(x2d, g, w_router, b_router, cfg, tiles):
    n, d = x2d.shape
    e = w_router.shape[1]
    assert e <= LANES and d % LANES == 0
    n_sub = d // LANES
    pitch = n_sub + SLAB_PAD
    tm = min(tiles.router_rows, n)
    wr = jnp.zeros((d, LANES), bf16).at[:, :e].set(w_router.astype(bf16))
    br = jnp.full((1, LANES), NEG_INF, f32).at[0, :e].set(b_router.astype(f32))
    hg, te, tw, cnt = pl.pallas_call(
        functools.partial(_router_kernel, tm=tm, n_sub=n_sub, top_k=cfg.top_k),
        grid=(n // tm,),
        in_specs=[pl.BlockSpec((tm, d), lambda i: (i, 0)),
                  pl.BlockSpec((1, d), lambda i: (0, 0)),
                  pl.BlockSpec((d, LANES), lambda i: (0, 0)),
                  pl.BlockSpec((1, LANES), lambda i: (0, 0))],
        out_specs=[pl.BlockSpec((tm * pitch, LANES), lambda i: (i, 0)),
                   pl.BlockSpec((tm, LANES), lambda i: (i, 0)),
                   pl.BlockSpec((tm, LANES), lambda i: (i, 0)),
                   pl.BlockSpec((SUBLANES, LANES), lambda i: (i, 0))],
        out_shape=[jax.ShapeDtypeStruct((n * pitch, LANES), f32),
                   jax.ShapeDtypeStruct((n, LANES), jnp.int32),
                   jax.ShapeDtypeStruct((n, LANES), f32),
                   jax.ShapeDtypeStruct((n // tm * SUBLANES, LANES), jnp.int32)],
        compiler_params=_params("parallel"),
        name="norm_router",
    )(x2d, g.reshape(1, d).astype(f32), wr, br)
    counts = jnp.sum(cnt, axis=0)[:e]
    return hg, te[:, :cfg.top_k], tw[:, :cfg.top_k], counts


def _routing_plan(top_e, gate_w, counts, blk):
    i32 = jnp.int32
    n_tok, top_k = top_e.shape
    n_experts = counts.shape[0]
    n_slot = n_tok * top_k
    slot_e = top_e.reshape(n_slot)
    slot_w = gate_w.reshape(n_slot)
    order = jnp.argsort(slot_e, stable=True).astype(i32)
    inv = jnp.argsort(order).astype(i32)
    experts = jnp.arange(n_experts, dtype=i32)
    padded = (counts + blk - 1) // blk * blk
    pad_end = jnp.cumsum(padded)
    pad_start = pad_end - padded
    start = jnp.cumsum(counts) - counts
    shift = pad_start - start
    n_blocks = -(-n_slot // blk) + n_experts
    n_rows = n_blocks * blk
    n_used = pad_end[-1] // blk
    last_e = jnp.max(jnp.where(counts > 0, experts, 0))
    block_start = jnp.arange(n_blocks, dtype=i32) * blk
    block_e = jnp.minimum(jnp.sum((pad_end[None, :] <= block_start[:, None]).astype(i32), axis=1), last_e)
    row = (block_start[:, None] + jnp.arange(blk, dtype=i32)[None, :])
    in_expert = row - pad_start[block_e][:, None]
    valid = (in_expert < counts[block_e][:, None]) & (row < pad_end[-1])
    row_slot = order[jnp.where(valid, row - shift[block_e][:, None], 0).reshape(n_rows)]
    valid = valid.reshape(n_rows)
    row_tok = jnp.where(valid, row_slot // top_k, 0)
    row_w = jnp.where(valid, slot_w[row_slot], 0.0)
    pos = inv + shift[slot_e]
    return row_tok, row_w.reshape(n_rows, 1), block_e, n_used.reshape(1).astype(i32), pos


def _gather_kernel(tok_ref, nu_ref, hg_hbm, o_ref, x3a_ref, x3b_ref, sem, *, blk, n_sub, unroll):
    b = pl.program_id(0)
    n_used = nu_ref[0]
    pitch = n_sub + SLAB_PAD
    bufs = (x3a_ref, x3b_ref)

    def issue(block_idx, slot):
        def body(r0, carry):
            for q in range(unroll):
                r = r0 * unroll + q
                tok = tok_ref[block_idx * blk + r]
                pltpu.make_async_copy(hg_hbm.at[pl.ds(tok * pitch, n_sub)], bufs[slot].at[:, r, :],
                                      sem.at[slot]).start()
            return carry
        lax.fori_loop(0, blk // unroll, body, 0)

    def wait_all(slot):
        pltpu.make_async_copy(bufs[slot], bufs[slot], sem.at[slot]).wait()

    @pl.when(b < n_used)
    def _():
        @pl.when(b == 0)
        def _():
            issue(0, 0)
        for slot in range(2):
            @pl.when(b % 2 == slot)
            def _():
                @pl.when(b + 1 < n_used)
                def _():
                    issue(b + 1, 1 - slot)
                wait_all(slot)
                for s in range(n_sub):
                    o_ref[:, s * LANES:(s + 1) * LANES] = bufs[slot][s].astype(o_ref.dtype)

    @pl.when(b >= n_used)
    def _():
        o_ref[...] = jnp.zeros(o_ref.shape, o_ref.dtype)


def _gate_up_kernel(be_ref, nu_ref, x_ref, wg_ref, bg_ref, wu_ref, bu_ref, o_ref):
    b = pl.program_id(1)

    @pl.when(b < nu_ref[0])
    def _():
        x = x_ref[...]
        g = jnp.minimum(jnp.dot(x, wg_ref[...], preferred_element_type=f32) + bg_ref[...], SWIGLU_LIMIT)
        up = jnp.clip(jnp.dot(x, wu_ref[...], preferred_element_type=f32) + bu_ref[...],
                      -SWIGLU_LIMIT, SWIGLU_LIMIT)
        o_ref[...] = ((up + 1.0) * (g * jax.nn.sigmoid(SWIGLU_ALPHA * g))).astype(o_ref.dtype)

    @pl.when(b >= nu_ref[0])
    def _():
        o_ref[...] = jnp.zeros(o_ref.shape, o_ref.dtype)


def _down_kernel(be_ref, nu_ref, a_ref, wd_ref, bd_ref, rw_ref, o_ref, y_ref, *, blk, n_sub, tc):
    b = pl.program_id(0)
    d = n_sub * LANES

    @pl.when(b < nu_ref[0])
    def _():
        a = a_ref[...]
        rw = rw_ref[...]
        for c in range(d // tc):
            cs = slice(c * tc, (c + 1) * tc)
            y = jnp.dot(a, wd_ref[:, cs], preferred_element_type=f32) + bd_ref[:, cs]
            y_ref[:, cs] = y * rw
        _store_token_rows(o_ref, y_ref, blk, n_sub)

    @pl.when(b >= nu_ref[0])
    def _():
        o_ref[...] = jnp.zeros(o_ref.shape, o_ref.dtype)


def _experts(hg, row_tok, row_w, block_e, n_used, wg, bg, wu, bu, wd, bd, tiles):
    n_e, d, ff = wg.shape
    n_sub = d // LANES
    pitch = n_sub + SLAB_PAD
    blk = tiles.expert_rows
    n_rows = row_tok.shape[0]
    n_blocks = n_rows // blk
    tf = min(tiles.expert_cols, ff)
    nf = ff // tf
    xs = pl.pallas_call(
        functools.partial(_gather_kernel, blk=blk, n_sub=n_sub, unroll=math.gcd(blk, tiles.gather_unroll)),
        grid_spec=pltpu.PrefetchScalarGridSpec(
            num_scalar_prefetch=2,
            grid=(n_blocks,),
            in_specs=[pl.BlockSpec(memory_space=pl.ANY)],
            out_specs=pl.BlockSpec((blk, d), lambda b, tok, nu: (b, 0)),
            scratch_shapes=[pltpu.VMEM((n_sub, blk, LANES), f32), pltpu.VMEM((n_sub, blk, LANES), f32),
                            pltpu.SemaphoreType.DMA((2,))]),
        out_shape=jax.ShapeDtypeStruct((n_rows, d), bf16),
        compiler_params=_params("arbitrary"),
        name="expert_gather",
    )(row_tok, n_used, hg)
    xb = lambda f, b, be, nu: (jnp.minimum(b, nu[0] - 1), 0)
    act = pl.pallas_call(
        _gate_up_kernel,
        grid_spec=pltpu.PrefetchScalarGridSpec(
            num_scalar_prefetch=2,
            grid=(nf, n_blocks),
            in_specs=[pl.BlockSpec((blk, d), xb),
                      pl.BlockSpec((None, d, tf), lambda f, b, be, nu: (be[b], 0, f)),
                      pl.BlockSpec((None, 1, tf), lambda f, b, be, nu: (be[b], 0, f)),
                      pl.BlockSpec((None, d, tf), lambda f, b, be, nu: (be[b], 0, f)),
                      pl.BlockSpec((None, 1, tf), lambda f, b, be, nu: (be[b], 0, f))],
            out_specs=pl.BlockSpec((blk, tf), lambda f, b, be, nu: (b, f))),
        out_shape=jax.ShapeDtypeStruct((n_rows, ff), bf16),
        compiler_params=_params("arbitrary", "arbitrary"),
        name="expert_gate_up",
    )(block_e, n_used, xs, wg, bg.reshape(n_e, 1, ff).astype(f32), wu,
      bu.reshape(n_e, 1, ff).astype(f32))
    tc = min(tiles.down_cols, d)
    return pl.pallas_call(
        functools.partial(_down_kernel, blk=blk, n_sub=n_sub, tc=tc),
        grid_spec=pltpu.PrefetchScalarGridSpec(
            num_scalar_prefetch=2,
            grid=(n_blocks,),
            in_specs=[pl.BlockSpec((blk, ff), lambda b, be, nu: (b, 0)),
                      pl.BlockSpec((None, ff, d), lambda b, be, nu: (be[b], 0, 0)),
                      pl.BlockSpec((None, 1, d), lambda b, be, nu: (be[b], 0, 0)),
                      pl.BlockSpec((blk, 1), lambda b, be, nu: (b, 0))],
            out_specs=pl.BlockSpec((blk * pitch, LANES), lambda b, be, nu: (b, 0)),
            scratch_shapes=[pltpu.VMEM((blk, d), f32)]),
        out_shape=jax.ShapeDtypeStruct((n_rows * pitch, LANES), f32),
        compiler_params=_params("arbitrary"),
        name="expert_down",
    )(block_e, n_used, act, wd, bd.reshape(n_e, 1, d).astype(f32), row_w)


def _combine_kernel(pos_ref, x_ref, ys_hbm, g_ref, o_ref, b0_ref, b1_ref, sem, *, tm, n_sub, top_k,
                    final_norm):
    i = pl.program_id(0)
    nb = pl.num_programs(0)
    bufs = (b0_ref, b1_ref)
    pitch = n_sub + SLAB_PAD

    def issue(block_idx, slot):
        def body(t0, carry):
            for q in range(2):
                t = t0 * 2 + q
                for k in range(top_k):
                    row = pos_ref[(block_idx * tm + t) * top_k + k]
                    pltpu.make_async_copy(ys_hbm.at[pl.ds(row * pitch, n_sub)],
                                          bufs[slot].at[:, k * tm + t, :], sem.at[slot]).start()
            return carry
        lax.fori_loop(0, tm // 2, body, 0)

    def wait_all(slot):
        pltpu.make_async_copy(bufs[slot], bufs[slot], sem.at[slot]).wait()

    @pl.when(i == 0)
    def _():
        issue(0, 0)

    for slot in range(2):
        @pl.when(i % 2 == slot)
        def _():
            @pl.when(i + 1 < nb)
            def _():
                issue(i + 1, 1 - slot)
            wait_all(slot)
            for s in range(n_sub):
                cs = slice(s * LANES, (s + 1) * LANES)
                acc = x_ref[:, cs]
                for k in range(top_k):
                    acc = acc + bufs[slot][s, k * tm:(k + 1) * tm, :]
                o_ref[:, cs] = acc

    if final_norm:
        x = o_ref[...]
        ms = jnp.mean(x * x, axis=-1, keepdims=True)
        o_ref[...] = x * lax.rsqrt(ms + RMS_EPS) * g_ref[...]


def _combine(x2d, ys, pos, g_final, final_norm, cfg, tiles):
    n, d = x2d.shape
    n_sub = d // LANES
    tm = min(tiles.combine_rows, n)
    top_k = cfg.top_k
    grid_spec = pltpu.PrefetchScalarGridSpec(
        num_scalar_prefetch=1,
        grid=(n // tm,),
        in_specs=[pl.BlockSpec((tm, d), lambda i, pos: (i, 0)),
                  pl.BlockSpec(memory_space=pl.ANY),
                  pl.BlockSpec((1, d), lambda i, pos: (0, 0))],
        out_specs=pl.BlockSpec((tm, d), lambda i, pos: (i, 0)),
        scratch_shapes=[pltpu.VMEM((n_sub, top_k * tm, LANES), f32),
                        pltpu.VMEM((n_sub, top_k * tm, LANES), f32),
                        pltpu.SemaphoreType.DMA((2,))],
    )
    return pl.pallas_call(
        functools.partial(_combine_kernel, tm=tm, n_sub=n_sub, top_k=top_k, final_norm=final_norm),
        grid_spec=grid_spec,
        out_shape=jax.ShapeDtypeStruct((n, d), f32),
        compiler_params=_params("arbitrary"),
        name="moe_combine",
    )(pos, x2d, ys, g_final.reshape(1, d).astype(f32))


def _layer(x, mem, p, l, is_last, g_final, cfg, tiles):
    b, s, d = x.shape
    n = b * s
    m_len = mem.shape[1]
    diff_width = d // 2
    conv_ch = d - diff_width
    x2d = x.reshape(n, d)

    h = _rmsnorm(x2d, p['g_mix'][l], tiles.norm_rows)
    head_dim = p['lambda_q1'].shape[-1]
    in_cols = p['w_in'].shape[-1]
    q_scale = jnp.where(jnp.arange(in_cols) < diff_width, head_dim ** -0.5 * LOG2E, 1.0)
    proj = _matmul([h], p['w_in'][l].astype(bf16), None, bf16, tiles.mm_rows, tiles.mm_cols, "in_proj",
                   col_scale=q_scale)
    proj3 = proj.reshape(b, s, -1)
    lam_init = 0.8 - 0.6 * math.exp(-0.3 * l)
    a_out = _diff_attention(proj3, p['lambda_q1'][l], p['lambda_k1'][l], p['lambda_q2'][l],
                            p['lambda_k2'][l], p['g_subln'][l], lam_init, cfg, tiles)
    c_out = _conformer(proj3, 3 * diff_width, p['w_dw'][l], p['b_dw'][l], p['g_conv_ln'][l],
                       p['b_conv_ln'][l], p['w_conv_pw'][l].astype(bf16), cfg, tiles)
    assert diff_width == conv_ch
    x2d = _matmul([a_out.reshape(n, diff_width), c_out.reshape(n, conv_ch)],
                  p['w_out'][l].astype(bf16), x2d, f32, tiles.mm_rows, tiles.mm_cols_res, "out_proj")

    hc = _rmsnorm(x2d, p['g_cross'][l], tiles.norm_rows)
    mem_n = _rmsnorm(mem.reshape(b * m_len, d), p['g_mem'][l], tiles.norm_rows)
    qc = _matmul([hc], p['w_cq'][l].astype(bf16), None, bf16, tiles.mm_rows, tiles.mm_cols, "xq_proj")
    kc = _matmul([mem_n], p['w_ck'][l].astype(bf16), None, bf16, tiles.mm_rows, tiles.mm_cols, "xk_proj")
    vc = _matmul([mem_n], p['w_cv'][l].astype(bf16), None, bf16, tiles.mm_rows, tiles.mm_cols, "xv_proj")
    oc = _cross_attention(qc.reshape(b, s, d), kc.reshape(b, m_len, d), vc.reshape(b, m_len, d),
                          cfg, tiles)
    x2d = _matmul([oc.reshape(n, d)], p['w_co'][l].astype(bf16), x2d, f32, tiles.mm_rows,
                  tiles.mm_cols_res, "xo_proj")

    hg, top_e, gate_w, counts = _norm_router(x2d, p['g_ffn'][l], p['w_router'][l], p['b_router'][l],
                                             cfg, tiles)
    row_tok, row_w, block_e, n_used, pos = _routing_plan(top_e, gate_w, counts, tiles.expert_rows)
    ys = _experts(hg, row_tok, row_w, block_e, n_used, p['w_gate'][l].astype(bf16), p['b_gate'][l],
                  p['w_up'][l].astype(bf16), p['b_up'][l], p['w_down'][l].astype(bf16), p['b_down'][l],
                  tiles)
    x2d = _combine(x2d, ys, pos, g_final, is_last, cfg, tiles)
    return x2d.reshape(b, s, d)


def _forward(x, mem, params, g_final, cfg, tiles):
    depth = params['g_mix'].shape[0]
    for l in range(depth):
        x = _layer(x, mem, params, l, l == depth - 1, g_final, cfg, tiles)
    return x


def kernel(x, mem, g_mix, w_in, lambda_q1, lambda_k1, lambda_q2, lambda_k2, g_subln, w_dw, b_dw,
           g_conv_ln, b_conv_ln, w_conv_pw, w_out, g_cross, g_mem, w_cq, w_ck, w_cv, w_co, g_ffn,
           w_router, b_router, w_gate, b_gate, w_up, b_up, w_down, b_down, g_final):
    params = dict(g_mix=g_mix, w_in=w_in, lambda_q1=lambda_q1, lambda_k1=lambda_k1,
                  lambda_q2=lambda_q2, lambda_k2=lambda_k2, g_subln=g_subln, w_dw=w_dw, b_dw=b_dw,
                  g_conv_ln=g_conv_ln, b_conv_ln=b_conv_ln, w_conv_pw=w_conv_pw, w_out=w_out,
                  g_cross=g_cross, g_mem=g_mem, w_cq=w_cq, w_ck=w_ck, w_cv=w_cv, w_co=w_co,
                  g_ffn=g_ffn, w_router=w_router, b_router=b_router, w_gate=w_gate, b_gate=b_gate,
                  w_up=w_up, b_up=b_up, w_down=w_down, b_down=b_down)
    return _forward(x, mem, params, g_final, Cfg(), Tiles())
```

```python
import functools
import math
from typing import NamedTuple

import jax
import jax.numpy as jnp
from jax import lax
from jax.experimental import pallas as pl
from jax.experimental.pallas import tpu as pltpu

f32 = jnp.float32
bf16 = jnp.bfloat16

RMS_EPS = 1e-6
LN_EPS = 1e-5
NEG_INF = -1e30
SWIGLU_LIMIT = 7.0
SWIGLU_ALPHA = 1.702
LOG2E = 1.4426950408889634

LANES = 128
SUBLANES = 8
SLAB_PAD = 1
VMEM_LIMIT_BYTES = 56 * 1024 * 1024


class Cfg(NamedTuple):
    diff_heads: int = 8
    conv_width: int = 31
    x_heads: int = 4
    top_k: int = 4


class Tiles(NamedTuple):
    norm_rows: int = 256
    mm_rows: int = 1024
    mm_cols: int = 512
    attn_q: int = 1024
    conv_rows: int = 256
    conv_halo: int = 32
    conv_acc_rows: int = 128
    xattn_q: int = 512
    router_rows: int = 256
    expert_rows: int = 256
    gather_unroll: int = 8
    expert_cols: int = 768
    down_cols: int = 512
    combine_rows: int = 128


def _params(*sem):
    return pltpu.CompilerParams(dimension_semantics=sem, vmem_limit_bytes=VMEM_LIMIT_BYTES)


def _rmsnorm_kernel(x_ref, g_ref, o_ref):
    x = x_ref[...]
    ms = jnp.mean(x * x, axis=-1, keepdims=True)
    o_ref[...] = (x * lax.rsqrt(ms + RMS_EPS) * g_ref[...]).astype(o_ref.dtype)


def _rmsnorm(x2d, g, rows):
    n, d = x2d.shape
    rows = min(rows, n)
    return pl.pallas_call(
        _rmsnorm_kernel,
        grid=(n // rows,),
        in_specs=[pl.BlockSpec((rows, d), lambda i: (i, 0)),
                  pl.BlockSpec((1, d), lambda i: (0, 0))],
        out_specs=pl.BlockSpec((rows, d), lambda i: (i, 0)),
        out_shape=jax.ShapeDtypeStruct((n, d), bf16),
        compiler_params=_params("parallel"),
        name="rmsnorm",
    )(x2d, g.reshape(1, d).astype(f32))


def _matmul_kernel(*refs, n_lhs, has_res, has_scale):
    a_refs = refs[:n_lhs]
    w_refs = refs[n_lhs:2 * n_lhs]
    extra = list(refs[2 * n_lhs:-1 - n_lhs])
    o_ref = refs[-1 - n_lhs]
    wb_refs = refs[len(refs) - n_lhs:]

    @pl.when(pl.program_id(1) == 0)
    def _():
        for w_ref, wb_ref in zip(w_refs, wb_refs):
            wb_ref[...] = w_ref[...].astype(bf16)

    acc = jnp.dot(a_refs[0][...], wb_refs[0][...], preferred_element_type=f32)
    for p in range(1, n_lhs):
        acc = acc + jnp.dot(a_refs[p][...], wb_refs[p][...], preferred_element_type=f32)
    if has_res:
        acc = acc + extra.pop(0)[...]
    if has_scale:
        acc = acc * extra.pop(0)[...]
    o_ref[...] = acc.astype(o_ref.dtype)


def _matmul(lhs_list, w, res, out_dtype, rows, cols, name, col_scale=None):
    n_lhs = len(lhs_list)
    m, kp = lhs_list[0].shape
    nc = w.shape[1]
    assert w.shape[0] == n_lhs * kp
    rows = min(rows, m)
    cols = min(cols, nc)
    in_specs = [pl.BlockSpec((rows, kp), lambda j, i: (i, 0)) for _ in range(n_lhs)]
    in_specs += [pl.BlockSpec((kp, cols), lambda j, i, p=p: (p, j)) for p in range(n_lhs)]
    args = list(lhs_list) + [w] * n_lhs
    if res is not None:
        in_specs.append(pl.BlockSpec((rows, cols), lambda j, i: (i, j)))
        args.append(res)
    if col_scale is not None:
        in_specs.append(pl.BlockSpec((1, cols), lambda j, i: (0, j)))
        args.append(col_scale.reshape(1, nc).astype(f32))
    return pl.pallas_call(
        functools.partial(_matmul_kernel, n_lhs=n_lhs, has_res=res is not None,
                          has_scale=col_scale is not None),
        grid=(nc // cols, m // rows),
        in_specs=in_specs,
        out_specs=pl.BlockSpec((rows, cols), lambda j, i: (i, j)),
        out_shape=jax.ShapeDtypeStruct((m, nc), out_dtype),
        scratch_shapes=[pltpu.VMEM((kp, cols), bf16) for _ in range(n_lhs)],
        compiler_params=_params("parallel", "arbitrary"),
        name=name,
    )(*args)


def _bf16_part(x):
    bits = lax.bitcast_convert_type(x, jnp.uint32) & jnp.uint32(0xFFFF0000)
    return lax.bitcast_convert_type(bits, f32)


def _lane_tile(x, n):
    return x if n == 1 else jnp.concatenate([x] * n, axis=1)


def _diff_attn_kernel(slopes_ref, q_ref, k_ref, v_ref, lq1_ref, lk1_ref, lq2_ref, lk2_ref, g_ref,
                      o_ref, qa1_ref, qa2_ref, ka1_ref, ka2_ref, acc1_ref, acc2_ref,
                      m1_ref, l1_ref, m2_ref, l2_ref, *, tq, d, lam_init):
    h = pl.program_id(1)
    qi = pl.program_id(2)
    tk = tq
    slope2 = slopes_ref[h]
    lane_q = lax.broadcasted_iota(jnp.int32, (tq, d), 1)
    ones_cols = jnp.where(lane_q < 3, 1.0, 0.0).astype(bf16)
    q = q_ref[...]
    qa1_ref[:, :d] = q[:, :d]
    qa1_ref[:, d:] = ones_cols
    qa2_ref[:, :d] = q[:, d:]
    qa2_ref[:, d:] = ones_cols
    lane_k = lax.broadcasted_iota(jnp.int32, (tk, d), 1)
    v0 = slope2 * lax.broadcasted_iota(jnp.int32, (tk, d), 0).astype(f32)
    hi = _bf16_part(v0)
    r1 = v0 - hi
    lo = _bf16_part(r1)
    lo2 = r1 - lo
    bias_cols = jnp.where(lane_k == 0, hi, jnp.where(lane_k == 1, lo, jnp.where(lane_k == 2, lo2, 0.0)))
    ka1_ref[:, d:] = bias_cols.astype(bf16)
    ka2_ref[:, d:] = bias_cols.astype(bf16)
    rel = (lax.broadcasted_iota(jnp.int32, (tq, tk), 0)
           - lax.broadcasted_iota(jnp.int32, (tq, tk), 1))
    maps = ((qa1_ref, ka1_ref, m1_ref, l1_ref, acc1_ref), (qa2_ref, ka2_ref, m2_ref, l2_ref, acc2_ref))
    for _, _, m_ref, l_ref, acc_ref in maps:
        m_ref[...] = jnp.full(m_ref.shape, NEG_INF, f32)
        l_ref[...] = jnp.zeros(l_ref.shape, f32)
        acc_ref[...] = jnp.zeros(acc_ref.shape, f32)

    def block(j, masked):
        k0 = pl.multiple_of(j * tk, tk)
        kb = k_ref[pl.ds(k0, tk), :]
        vb = v_ref[pl.ds(k0, tk), :]
        cj = slope2 * ((j - qi) * tk).astype(f32)
        for c, (qa_ref, ka_ref, m_ref, l_ref, acc_ref) in enumerate(maps):
            ka_ref[:, :d] = kb[:, c * d:(c + 1) * d]
            s = lax.dot_general(qa_ref[...], ka_ref[...], (((1,), (1,)), ((), ())),
                                preferred_element_type=f32)
            if masked:
                s = jnp.where(rel >= 0, s, NEG_INF)
            m_old = m_ref[...]
            m_new = jnp.maximum(m_old, jnp.max(s, axis=-1, keepdims=True) + cj)
            p = jnp.exp2(s - _lane_tile(m_new - cj, tk // LANES))
            alpha = jnp.exp2(m_old - m_new)
            l_ref[...] = alpha * l_ref[...] + jnp.sum(p, axis=-1, keepdims=True)
            acc_ref[...] = (_lane_tile(alpha, acc_ref.shape[1] // LANES) * acc_ref[...]
                            + jnp.dot(p.astype(bf16), vb, preferred_element_type=f32))
            m_ref[...] = m_new

    def off_diag(j, carry):
        block(j, False)
        return carry

    lax.fori_loop(0, qi, off_diag, 0)
    block(qi, True)

    lam = (jnp.exp(jnp.sum(lq1_ref[...] * lk1_ref[...], axis=-1, keepdims=True))
           - jnp.exp(jnp.sum(lq2_ref[...] * lk2_ref[...], axis=-1, keepdims=True)) + lam_init)
    reps = acc1_ref.shape[1] // LANES
    o = (acc1_ref[...] / _lane_tile(l1_ref[...], reps)
         - lam * (acc2_ref[...] / _lane_tile(l2_ref[...], reps)))
    ms = jnp.mean(o * o, axis=-1, keepdims=True)
    y = o * lax.rsqrt(ms + RMS_EPS) * g_ref[...] * (1.0 - lam_init)
    o_ref[...] = y.astype(o_ref.dtype)


def _diff_attention(proj3, lq1, lk1, lq2, lk2, g_subln, lam_init, cfg, tiles):
    b, s, _ = proj3.shape
    nh = cfg.diff_heads
    d = lq1.shape[-1]
    hw = 2 * d
    tq = min(tiles.attn_q, s)
    slopes = jnp.exp2(-8.0 * jnp.arange(1, nh + 1, dtype=f32) / nh) * LOG2E
    vec = lambda a: a.reshape(1, -1).astype(f32)
    small = lambda n: pl.BlockSpec((1, n), lambda bi, hi, qi: (0, 0))
    return pl.pallas_call(
        functools.partial(_diff_attn_kernel, tq=tq, d=d, lam_init=lam_init),
        grid=(b, nh, s // tq),
        in_specs=[pl.BlockSpec(memory_space=pltpu.SMEM),
                  pl.BlockSpec((None, tq, hw), lambda bi, hi, qi: (bi, qi, hi)),
                  pl.BlockSpec((None, s, hw), lambda bi, hi, qi: (bi, 0, nh + hi)),
                  pl.BlockSpec((None, s, hw), lambda bi, hi, qi: (bi, 0, 2 * nh + hi)),
                  small(d), small(d), small(d), small(d), small(hw)],
        out_specs=pl.BlockSpec((None, tq, hw), lambda bi, hi, qi: (bi, qi, hi)),
        out_shape=jax.ShapeDtypeStruct((b, s, nh * hw), bf16),
        scratch_shapes=[pltpu.VMEM((tq, hw), bf16), pltpu.VMEM((tq, hw), bf16),
                        pltpu.VMEM((tq, hw), bf16), pltpu.VMEM((tq, hw), bf16),
                        pltpu.VMEM((tq, hw), f32), pltpu.VMEM((tq, hw), f32),
                        pltpu.VMEM((tq, LANES), f32), pltpu.VMEM((tq, LANES), f32),
                        pltpu.VMEM((tq, LANES), f32), pltpu.VMEM((tq, LANES), f32)],
        compiler_params=_params("parallel", "parallel", "parallel"),
        name="diff_attention",
    )(slopes, proj3, proj3, proj3, vec(lq1), vec(lk1), vec(lq2), vec(lk2), vec(g_subln))


def _conformer_kernel(a_ref, gate_ref, ah_ref, gateh_ref, wdw_ref, bdw_ref, gln_ref, bln_ref,
                      wpw_ref, o_ref, buf_ref, conv_ref, win_ref, *, ts, kw, halo, rt):
    i = pl.program_id(1)
    ch = a_ref.shape[-1]
    glu = a_ref[...].astype(f32) * jax.nn.sigmoid(gate_ref[...].astype(f32))
    glu_h = ah_ref[...].astype(f32) * jax.nn.sigmoid(gateh_ref[...].astype(f32))
    buf_ref[0:halo, :] = jnp.where(i > 0, glu_h, 0.0)
    buf_ref[halo:halo + ts, :] = glu
    off = halo - (kw - 1)
    def conv_chunk(c, carry):
        cs = pl.ds(pl.multiple_of(c * LANES, LANES), LANES)
        for r0 in range(0, ts, rt):
            acc = jnp.broadcast_to(bdw_ref[:, cs], (rt, LANES))
            for r in range(min(SUBLANES, kw)):
                span = (kw - 1 - r) // SUBLANES * SUBLANES
                win_ref[r, 0:rt + span, :] = buf_ref[r0 + off + r:r0 + off + r + rt + span, cs]
                for j in range(r, kw, SUBLANES):
                    acc = acc + wdw_ref[j:j + 1, cs] * win_ref[r, j - r:j - r + rt, :]
            conv_ref[r0:r0 + rt, cs] = acc
        return carry

    lax.fori_loop(0, ch // LANES, conv_chunk, 0)
    c = conv_ref[...]
    mu = jnp.mean(c, axis=-1, keepdims=True)
    xc = c - mu
    var = jnp.mean(xc * xc, axis=-1, keepdims=True)
    y = xc * lax.rsqrt(var + LN_EPS) * gln_ref[...] + bln_ref[...]
    act = y * jax.nn.sigmoid(y)
    o_ref[...] = jnp.dot(act.astype(bf16), wpw_ref[...],
                         preferred_element_type=f32).astype(o_ref.dtype)


def _conformer(proj3, col0, w_dw, b_dw, g_ln, b_ln, w_pw, cfg, tiles):
    b, s, _ = proj3.shape
    kw = cfg.conv_width
    ch = w_pw.shape[0]
    ts = min(tiles.conv_rows, s)
    halo = tiles.conv_halo
    assert halo >= kw - 1 and ts % halo == 0 and col0 % ch == 0
    cb = col0 // ch
    hb = ts // halo
    rt = math.gcd(ts, tiles.conv_acc_rows)
    win_rows = rt + (kw - 1) // SUBLANES * SUBLANES
    vec = lambda a: a.reshape(1, ch).astype(f32)
    small = lambda: pl.BlockSpec((1, ch), lambda bi, i: (0, 0))
    return pl.pallas_call(
        functools.partial(_conformer_kernel, ts=ts, kw=kw, halo=halo, rt=rt),
        grid=(b, s // ts),
        in_specs=[pl.BlockSpec((None, ts, ch), lambda bi, i: (bi, i, cb)),
                  pl.BlockSpec((None, ts, ch), lambda bi, i: (bi, i, cb + 1)),
                  pl.BlockSpec((None, halo, ch), lambda bi, i: (bi, jnp.maximum(i * hb - 1, 0), cb)),
                  pl.BlockSpec((None, halo, ch), lambda bi, i: (bi, jnp.maximum(i * hb - 1, 0), cb + 1)),
                  pl.BlockSpec((kw, ch), lambda bi, i: (0, 0)),
                  small(), small(), small(),
                  pl.BlockSpec((ch, ch), lambda bi, i: (0, 0))],
        out_specs=pl.BlockSpec((None, ts, ch), lambda bi, i: (bi, i, 0)),
        out_shape=jax.ShapeDtypeStruct((b, s, ch), bf16),
        scratch_shapes=[pltpu.VMEM((halo + ts, ch), f32), pltpu.VMEM((ts, ch), f32),
                        pltpu.VMEM((SUBLANES, win_rows, LANES), f32)],
        compiler_params=_params("parallel", "parallel"),
        name="conformer",
    )(proj3, proj3, proj3, proj3, w_dw.reshape(kw, ch).astype(f32), vec(b_dw), vec(g_ln), vec(b_ln),
      w_pw)


def _xattn_kernel(q_ref, k_ref, v_ref, o_ref, *, scale):
    s = lax.dot_general(q_ref[...], k_ref[...], (((1,), (1,)), ((), ())),
                        preferred_element_type=f32) * scale
    m = jnp.max(s, axis=-1, keepdims=True)
    p = jnp.exp(s - m)
    p = p / jnp.sum(p, axis=-1, keepdims=True)
    o_ref[...] = jnp.dot(p.astype(bf16), v_ref[...], preferred_element_type=f32).astype(o_ref.dtype)


def _cross_attention(q3, k3, v3, cfg, tiles):
    b, s, dm = q3.shape
    m = k3.shape[1]
    nh = cfg.x_heads
    hd = dm // nh
    tq = min(tiles.xattn_q, s)
    return pl.pallas_call(
        functools.partial(_xattn_kernel, scale=hd ** -0.5),
        grid=(b, s // tq, nh),
        in_specs=[pl.BlockSpec((None, tq, hd), lambda bi, i, h: (bi, i, h)),
                  pl.BlockSpec((None, m, hd), lambda bi, i, h: (bi, 0, h)),
                  pl.BlockSpec((None, m, hd), lambda bi, i, h: (bi, 0, h))],
        out_specs=pl.BlockSpec((None, tq, hd), lambda bi, i, h: (bi, i, h)),
        out_shape=jax.ShapeDtypeStruct((b, s, dm), bf16),
        compiler_params=_params("parallel", "parallel", "parallel"),
        name="cross_attention",
    )(q3, k3, v3)


def _store_token_rows(o_ref, y, n_rows, n_sub):
    pitch = n_sub + SLAB_PAD
    for s in range(n_sub):
        o_ref[pl.ds(s, n_rows, stride=pitch), :] = y[:, s * LANES:(s + 1) * LANES]
    o_ref[pl.ds(n_sub, n_rows, stride=pitch), :] = jnp.zeros((n_rows, LANES), o_ref.dtype)


def _router_kernel(x_ref, g_ref, wr_ref, br_ref, hg_ref, te_ref, tw_ref, cnt_ref, *, tm, n_sub, top_k):
    x = x_ref[...]
    ms = jnp.mean(x * x, axis=-1, keepdims=True)
    h = x * lax.rsqrt(ms + RMS_EPS) * g_ref[...]
    _store_token_rows(hg_ref, h, tm, n_sub)
    logits = jnp.dot(h.astype(bf16), wr_ref[...], preferred_element_type=f32) + br_ref[...]
    lane = lax.broadcasted_iota(jnp.int32, logits.shape, 1)
    lane_f = lane.astype(f32)
    vals, idxs = [], []
    for _ in range(top_k):
        m = jnp.max(logits, axis=-1, keepdims=True)
        idx = jnp.min(jnp.where(logits == m, lane_f, float(LANES)), axis=-1, keepdims=True)
        vals.append(m)
        idxs.append(idx)
        logits = jnp.where(lane_f == idx, NEG_INF * 2.0, logits)
    exps = [jnp.exp(v - vals[0]) for v in vals]
    denom = exps[0]
    for e in exps[1:]:
        denom = denom + e
    te = jnp.zeros(logits.shape, f32)
    tw = jnp.zeros(logits.shape, f32)
    hits = jnp.zeros(logits.shape, f32)
    for k in range(top_k):
        te = jnp.where(lane == k, idxs[k], te)
        tw = jnp.where(lane == k, exps[k] / denom, tw)
        hits = hits + jnp.where(lane_f == idxs[k], 1.0, 0.0)
    te_ref[...] = te.astype(jnp.int32)
    tw_ref[...] = tw
    tile_counts = jnp.sum(hits, axis=0, keepdims=True)
    row8 = lax.broadcasted_iota(jnp.int32, cnt_ref.shape, 0)
    cnt_ref[...] = jnp.where(row8 == 0, tile_counts, 0.0).astype(jnp.int32)


def _norm_router(x2d, g, w_router, b_router, cfg, tiles):
    n, d = x2d.shape
    e = w_router.shape[1]
    assert e <= LANES and d % LANES == 0
    n_sub = d // LANES
    pitch = n_sub + SLAB_PAD
    tm = min(tiles.router_rows, n)
    wr = jnp.zeros((d, LANES), bf16).at[:, :e].set(w_router.astype(bf16))
    br = jnp.full((1, LANES), NEG_INF, f32).at[0, :e].set(b_router.astype(f32))
    hg, te, tw, cnt = pl.pallas_call(
        functools.partial(_router_kernel, tm=tm, n_sub=n_sub, top_k=cfg.top_k),
        grid=(n // tm,),
        in_specs=[pl.BlockSpec((tm, d), lambda i: (i, 0)),
                  pl.BlockSpec((1, d), lambda i: (0, 0)),
                  pl.BlockSpec((d, LANES), lambda i: (0, 0)),
                  pl.BlockSpec((1, LANES), lambda i: (0, 0))],
        out_specs=[pl.BlockSpec((tm * pitch, LANES), lambda i: (i, 0)),
                   pl.BlockSpec((tm, LANES), lambda i: (i, 0)),
                   pl.BlockSpec((tm, LANES), lambda i: (i, 0)),
                   pl.BlockSpec((SUBLANES, LANES), lambda i: (i, 0))],
        out_shape=[jax.ShapeDtypeStruct((n * pitch, LANES), f32),
                   jax.ShapeDtypeStruct((n, LANES), jnp.int32),
                   jax.ShapeDtypeStruct((n, LANES), f32),
                   jax.ShapeDtypeStruct((n // tm * SUBLANES, LANES), jnp.int32)],
        compiler_params=_params("parallel"),
        name="norm_router",
    )(x2d, g.reshape(1, d).astype(f32), wr, br)
    counts = jnp.sum(cnt, axis=0)[:e]
    return hg, te[:, :cfg.top_k], tw[:, :cfg.top_k], counts


def _routing_plan(top_e, gate_w, counts, blk):
    i32 = jnp.int32
    n_tok, top_k = top_e.shape
    n_experts = counts.shape[0]
    n_slot = n_tok * top_k
    slot_e = top_e.reshape(n_slot)
    slot_w = gate_w.reshape(n_slot)
    order = jnp.argsort(slot_e, stable=True).astype(i32)
    inv = jnp.argsort(order).astype(i32)
    experts = jnp.arange(n_experts, dtype=i32)
    padded = (counts + blk - 1) // blk * blk
    pad_end = jnp.cumsum(padded)
    pad_start = pad_end - padded
    start = jnp.cumsum(counts) - counts
    shift = pad_start - start
    n_blocks = -(-n_slot // blk) + n_experts
    n_rows = n_blocks * blk
    n_used = pad_end[-1] // blk
    last_e = jnp.max(jnp.where(counts > 0, experts, 0))
    block_start = jnp.arange(n_blocks, dtype=i32) * blk
    block_e = jnp.minimum(jnp.sum((pad_end[None, :] <= block_start[:, None]).astype(i32), axis=1), last_e)
    row = (block_start[:, None] + jnp.arange(blk, dtype=i32)[None, :])
    in_expert = row - pad_start[block_e][:, None]
    valid = (in_expert < counts[block_e][:, None]) & (row < pad_end[-1])
    row_slot = order[jnp.where(valid, row - shift[block_e][:, None], 0).reshape(n_rows)]
    valid = valid.reshape(n_rows)
    row_tok = jnp.where(valid, row_slot // top_k, 0)
    row_w = jnp.where(valid, slot_w[row_slot], 0.0)
    pos = inv + shift[slot_e]
    return row_tok, row_w.reshape(n_rows, 1), block_e, n_used.reshape(1).astype(i32), pos


def _gather_kernel(tok_ref, nu_ref, hg_hbm, o_ref, x3a_ref, x3b_ref, sem, *, blk, n_sub, unroll):
    b = pl.program_id(0)
    n_used = nu_ref[0]
    pitch = n_sub + SLAB_PAD
    bufs = (x3a_ref, x3b_ref)

    def issue(block_idx, slot):
        def body(r0, carry):
            for q in range(unroll):
                r = r0 * unroll + q
                tok = tok_ref[block_idx * blk + r]
                pltpu.make_async_copy(hg_hbm.at[pl.ds(tok * pitch, n_sub)], bufs[slot].at[:, r, :],
                                      sem.at[slot]).start()
            return carry
        lax.fori_loop(0, blk // unroll, body, 0)

    def wait_all(slot):
        pltpu.make_async_copy(bufs[slot], bufs[slot], sem.at[slot]).wait()

    @pl.when(b < n_used)
    def _():
        @pl.when(b == 0)
        def _():
            issue(0, 0)
        for slot in range(2):
            @pl.when(b % 2 == slot)
            def _():
                @pl.when(b + 1 < n_used)
                def _():
                    issue(b + 1, 1 - slot)
                wait_all(slot)
                for s in range(n_sub):
                    o_ref[:, s * LANES:(s + 1) * LANES] = bufs[slot][s].astype(o_ref.dtype)

    @pl.when(b >= n_used)
    def _():
        o_ref[...] = jnp.zeros(o_ref.shape, o_ref.dtype)


def _gate_up_kernel(be_ref, nu_ref, x_ref, wg_ref, bg_ref, wu_ref, bu_ref, o_ref):
    b = pl.program_id(1)

    @pl.when(b < nu_ref[0])
    def _():
        x = x_ref[...]
        g = jnp.minimum(jnp.dot(x, wg_ref[...], preferred_element_type=f32) + bg_ref[...], SWIGLU_LIMIT)
        up = jnp.clip(jnp.dot(x, wu_ref[...], preferred_element_type=f32) + bu_ref[...],
                      -SWIGLU_LIMIT, SWIGLU_LIMIT)
        o_ref[...] = ((up + 1.0) * (g * jax.nn.sigmoid(SWIGLU_ALPHA * g))).astype(o_ref.dtype)

    @pl.when(b >= nu_ref[0])
    def _():
        o_ref[...] = jnp.zeros(o_ref.shape, o_ref.dtype)


def _down_kernel(be_ref, nu_ref, a_ref, wd_ref, bd_ref, rw_ref, o_ref, y_ref, *, blk, n_sub, tc):
    b = pl.program_id(0)
    d = n_sub * LANES

    @pl.when(b < nu_ref[0])
    def _():
        a = a_ref[...]
        rw = rw_ref[...]
        for c in range(d // tc):
            cs = slice(c * tc, (c + 1) * tc)
            y = jnp.dot(a, wd_ref[:, cs], preferred_element_type=f32) + bd_ref[:, cs]
            y_ref[:, cs] = y * rw
        _store_token_rows(o_ref, y_ref, blk, n_sub)

    @pl.when(b >= nu_ref[0])
    def _():
        o_ref[...] = jnp.zeros(o_ref.shape, o_ref.dtype)


def _experts(hg, row_tok, row_w, block_e, n_used, wg, bg, wu, bu, wd, bd, tiles):
    n_e, d, ff = wg.shape
    n_sub = d // LANES
    pitch = n_sub + SLAB_PAD
    blk = tiles.expert_rows
    n_rows = row_tok.shape[0]
    n_blocks = n_rows // blk
    tf = min(tiles.expert_cols, ff)
    nf = ff // tf
    xs = pl.pallas_call(
        functools.partial(_gather_kernel, blk=blk, n_sub=n_sub, unroll=math.gcd(blk, tiles.gather_unroll)),
        grid_spec=pltpu.PrefetchScalarGridSpec(
            num_scalar_prefetch=2,
            grid=(n_blocks,),
            in_specs=[pl.BlockSpec(memory_space=pl.ANY)],
            out_specs=pl.BlockSpec((blk, d), lambda b, tok, nu: (b, 0)),
            scratch_shapes=[pltpu.VMEM((n_sub, blk, LANES), f32), pltpu.VMEM((n_sub, blk, LANES), f32),
                            pltpu.SemaphoreType.DMA((2,))]),
        out_shape=jax.ShapeDtypeStruct((n_rows, d), bf16),
        compiler_params=_params("arbitrary"),
        name="expert_gather",
    )(row_tok, n_used, hg)
    xb = lambda f, b, be, nu: (jnp.minimum(b, nu[0] - 1), 0)
    act = pl.pallas_call(
        _gate_up_kernel,
        grid_spec=pltpu.PrefetchScalarGridSpec(
            num_scalar_prefetch=2,
            grid=(nf, n_blocks),
            in_specs=[pl.BlockSpec((blk, d), xb),
                      pl.BlockSpec((None, d, tf), lambda f, b, be, nu: (be[b], 0, f)),
                      pl.BlockSpec((None, 1, tf), lambda f, b, be, nu: (be[b], 0, f)),
                      pl.BlockSpec((None, d, tf), lambda f, b, be, nu: (be[b], 0, f)),
                      pl.BlockSpec((None, 1, tf), lambda f, b, be, nu: (be[b], 0, f))],
            out_specs=pl.BlockSpec((blk, tf), lambda f, b, be, nu: (b, f))),
        out_shape=jax.ShapeDtypeStruct((n_rows, ff), bf16),
        compiler_params=_params("arbitrary", "arbitrary"),
        name="expert_gate_up",
    )(block_e, n_used, xs, wg, bg.reshape(n_e, 1, ff).astype(f32), wu,
      bu.reshape(n_e, 1, ff).astype(f32))
    tc = min(tiles.down_cols, d)
    return pl.pallas_call(
        functools.partial(_down_kernel, blk=blk, n_sub=n_sub, tc=tc),
        grid_spec=pltpu.PrefetchScalarGridSpec(
            num_scalar_prefetch=2,
            grid=(n_blocks,),
            in_specs=[pl.BlockSpec((blk, ff), lambda b, be, nu: (b, 0)),
                      pl.BlockSpec((None, ff, d), lambda b, be, nu: (be[b], 0, 0)),
                      pl.BlockSpec((None, 1, d), lambda b, be, nu: (be[b], 0, 0)),
                      pl.BlockSpec((blk, 1), lambda b, be, nu: (b, 0))],
            out_specs=pl.BlockSpec((blk * pitch, LANES), lambda b, be, nu: (b, 0)),
            scratch_shapes=[pltpu.VMEM((blk, d), f32)]),
        out_shape=jax.ShapeDtypeStruct((n_rows * pitch, LANES), f32),
        compiler_params=_params("arbitrary"),
        name="expert_down",
    )(block_e, n_used, act, wd, bd.reshape(n_e, 1, d).astype(f32), row_w)


def _combine_kernel(pos_ref, x_ref, ys_hbm, g_ref, o_ref, b0_ref, b1_ref, sem, *, tm, n_sub, top_k,
                    final_norm):
    i = pl.program_id(0)
    nb = pl.num_programs(0)
    bufs = (b0_ref, b1_ref)
    pitch = n_sub + SLAB_PAD

    def issue(block_idx, slot):
        def body(t0, carry):
            for q in range(2):
                t = t0 * 2 + q
                for k in range(top_k):
                    row = pos_ref[(block_idx * tm + t) * top_k + k]
                    pltpu.make_async_copy(ys_hbm.at[pl.ds(row * pitch, n_sub)],
                                          bufs[slot].at[:, k * tm + t, :], sem.at[slot]).start()
            return carry
        lax.fori_loop(0, tm // 2, body, 0)

    def wait_all(slot):
        pltpu.make_async_copy(bufs[slot], bufs[slot], sem.at[slot]).wait()

    @pl.when(i == 0)
    def _():
        issue(0, 0)

    for slot in range(2):
        @pl.when(i % 2 == slot)
        def _():
            @pl.when(i + 1 < nb)
            def _():
                issue(i + 1, 1 - slot)
            wait_all(slot)
            for s in range(n_sub):
                cs = slice(s * LANES, (s + 1) * LANES)
                acc = x_ref[:, cs]
                for k in range(top_k):
                    acc = acc + bufs[slot][s, k * tm:(k + 1) * tm, :]
                o_ref[:, cs] = acc

    if final_norm:
        x = o_ref[...]
        ms = jnp.mean(x * x, axis=-1, keepdims=True)
        o_ref[...] = x * lax.rsqrt(ms + RMS_EPS) * g_ref[...]


def _combine(x2d, ys, pos, g_final, final_norm, cfg, tiles):
    n, d = x2d.shape
    n_sub = d // LANES
    tm = min(tiles.combine_rows, n)
    top_k = cfg.top_k
    grid_spec = pltpu.PrefetchScalarGridSpec(
        num_scalar_prefetch=1,
        grid=(n // tm,),
        in_specs=[pl.BlockSpec((tm, d), lambda i, pos: (i, 0)),
                  pl.BlockSpec(memory_space=pl.ANY),
                  pl.BlockSpec((1, d), lambda i, pos: (0, 0))],
        out_specs=pl.BlockSpec((tm, d), lambda i, pos: (i, 0)),
        scratch_shapes=[pltpu.VMEM((n_sub, top_k * tm, LANES), f32),
                        pltpu.VMEM((n_sub, top_k * tm, LANES), f32),
                        pltpu.SemaphoreType.DMA((2,))],
    )
    return pl.pallas_call(
        functools.partial(_combine_kernel, tm=tm, n_sub=n_sub, top_k=top_k, final_norm=final_norm),
        grid_spec=grid_spec,
        out_shape=jax.ShapeDtypeStruct((n, d), f32),
        compiler_params=_params("arbitrary"),
        name="moe_combine",
    )(pos, x2d, ys, g_final.reshape(1, d).astype(f32))


def _layer(x, mem, p, l, is_last, g_final, cfg, tiles):
    b, s, d = x.shape
    n = b * s
    m_len = mem.shape[1]
    diff_width = d // 2
    conv_ch = d - diff_width
    x2d = x.reshape(n, d)

    h = _rmsnorm(x2d, p['g_mix'][l], tiles.norm_rows)
    head_dim = p['lambda_q1'].shape[-1]
    in_cols = p['w_in'].shape[-1]
    q_scale = jnp.where(jnp.arange(in_cols) < diff_width, head_dim ** -0.5 * LOG2E, 1.0)
    proj = _matmul([h], p['w_in'][l], None, bf16, tiles.mm_rows, tiles.mm_cols, "in_proj",
                   col_scale=q_scale)
    proj3 = proj.reshape(b, s, -1)
    lam_init = 0.8 - 0.6 * math.exp(-0.3 * l)
    a_out = _diff_attention(proj3, p['lambda_q1'][l], p['lambda_k1'][l], p['lambda_q2'][l],
                            p['lambda_k2'][l], p['g_subln'][l], lam_init, cfg, tiles)
    c_out = _conformer(proj3, 3 * diff_width, p['w_dw'][l], p['b_dw'][l], p['g_conv_ln'][l],
                       p['b_conv_ln'][l], p['w_conv_pw'][l].astype(bf16), cfg, tiles)
    assert diff_width == conv_ch
    x2d = _matmul([a_out.reshape(n, diff_width), c_out.reshape(n, conv_ch)],
                  p['w_out'][l], x2d, f32, tiles.mm_rows, tiles.mm_cols, "out_proj")

    hc = _rmsnorm(x2d, p['g_cross'][l], tiles.norm_rows)
    mem_n = _rmsnorm(mem.reshape(b * m_len, d), p['g_mem'][l], tiles.norm_rows)
    qc = _matmul([hc], p['w_cq'][l], None, bf16, tiles.mm_rows, tiles.mm_cols, "xq_proj")
    kc = _matmul([mem_n], p['w_ck'][l], None, bf16, tiles.mm_rows, tiles.mm_cols, "xk_proj")
    vc = _matmul([mem_n], p['w_cv'][l], None, bf16, tiles.mm_rows, tiles.mm_cols, "xv_proj")
    oc = _cross_attention(qc.reshape(b, s, d), kc.reshape(b, m_len, d), vc.reshape(b, m_len, d),
                          cfg, tiles)
    x2d = _matmul([oc.reshape(n, d)], p['w_co'][l], x2d, f32, tiles.mm_rows, tiles.mm_cols, "xo_proj")

    hg, top_e, gate_w, counts = _norm_router(x2d, p['g_ffn'][l], p['w_router'][l], p['b_router'][l],
                                             cfg, tiles)
    row_tok, row_w, block_e, n_used, pos = _routing_plan(top_e, gate_w, counts, tiles.expert_rows)
    ys = _experts(hg, row_tok, row_w, block_e, n_used, p['w_gate'][l].astype(bf16), p['b_gate'][l],
                  p['w_up'][l].astype(bf16), p['b_up'][l], p['w_down'][l].astype(bf16), p['b_down'][l],
                  tiles)
    x2d = _combine(x2d, ys, pos, g_final, is_last, cfg, tiles)
    return x2d.reshape(b, s, d)


def _forward(x, mem, params, g_final, cfg, tiles):
    depth = params['g_mix'].shape[0]
    for l in range(depth):
        x = _layer(x, mem, params, l, l == depth - 1, g_final, cfg, tiles)
    return x


def kernel(x, mem, g_mix, w_in, lambda_q1, lambda_k1, lambda_q2, lambda_k2, g_subln, w_dw, b_dw,
           g_conv_ln, b_conv_ln, w_conv_pw, w_out, g_cross, g_mem, w_cq, w_ck, w_cv, w_co, g_ffn,
           w_router, b_router, w_gate, b_gate, w_up, b_up, w_down, b_down, g_final):
    params = dict(g_mix=g_mix, w_in=w_in, lambda_q1=lambda_q1, lambda_k1=lambda_k1,
                  lambda_q2=lambda_q2, lambda_k2=lambda_k2, g_subln=g_subln, w_dw=w_dw, b_dw=b_dw,
                  g_conv_ln=g_conv_ln, b_conv_ln=b_conv_ln, w_conv_pw=w_conv_pw, w_out=w_out,
                  g_cross=g_cross, g_mem=g_mem, w_cq=w_cq, w_ck=w_ck, w_cv=w_cv, w_co=w_co,
                  g_ffn=g_ffn, w_router=w_router, b_router=b_router, w_gate=w_gate, b_gate=b_gate,
                  w_up=w_up, b_up=b_up, w_down=w_down, b_down=b_down)
    return _forward(x, mem, params, g_final, Cfg(), Tiles())
```

```python
import functools
import math
from typing import NamedTuple

import jax
import jax.numpy as jnp
from jax import lax
from jax.experimental import pallas as pl
from jax.experimental.pallas import tpu as pltpu

f32 = jnp.float32
bf16 = jnp.bfloat16

RMS_EPS = 1e-6
LN_EPS = 1e-5
NEG_INF = -1e30
SWIGLU_LIMIT = 7.0
SWIGLU_ALPHA = 1.702
LOG2E = 1.4426950408889634

LANES = 128
SUBLANES = 8
SLAB_PAD = 1
VMEM_LIMIT_BYTES = 56 * 1024 * 1024


class Cfg(NamedTuple):
    diff_heads: int = 8
    conv_width: int = 31
    x_heads: int = 4
    top_k: int = 4


class Tiles(NamedTuple):
    norm_rows: int = 256
    mm_rows: int = 1024
    mm_cols: int = 512
    attn_q: int = 1024
    conv_rows: int = 256
    conv_halo: int = 32
    conv_acc_rows: int = 128
    xattn_q: int = 512
    router_rows: int = 256
    expert_rows: int = 256
    gather_unroll: int = 8
    expert_cols: int = 768
    down_cols: int = 512
    combine_rows: int = 128


def _params(*sem):
    return pltpu.CompilerParams(dimension_semantics=sem, vmem_limit_bytes=VMEM_LIMIT_BYTES)


def _rmsnorm_kernel(x_ref, g_ref, o_ref):
    x = x_ref[...]
    ms = jnp.mean(x * x, axis=-1, keepdims=True)
    o_ref[...] = (x * lax.rsqrt(ms + RMS_EPS) * g_ref[...]).astype(o_ref.dtype)


def _rmsnorm(x2d, g, rows):
    n, d = x2d.shape
    rows = min(rows, n)
    return pl.pallas_call(
        _rmsnorm_kernel,
        grid=(n // rows,),
        in_specs=[pl.BlockSpec((rows, d), lambda i: (i, 0)),
                  pl.BlockSpec((1, d), lambda i: (0, 0))],
        out_specs=pl.BlockSpec((rows, d), lambda i: (i, 0)),
        out_shape=jax.ShapeDtypeStruct((n, d), bf16),
        compiler_params=_params("parallel"),
        name="rmsnorm",
    )(x2d, g.reshape(1, d).astype(f32))


def _matmul_kernel(*refs, n_lhs, has_res, has_scale):
    a_refs = refs[:n_lhs]
    w_refs = refs[n_lhs:2 * n_lhs]
    extra = list(refs[2 * n_lhs:-1 - n_lhs])
    o_ref = refs[-1 - n_lhs]
    wb_refs = refs[len(refs) - n_lhs:]

    @pl.when(pl.program_id(1) == 0)
    def _():
        for w_ref, wb_ref in zip(w_refs, wb_refs):
            wb_ref[...] = w_ref[...].astype(bf16)

    acc = jnp.dot(a_refs[0][...], wb_refs[0][...], preferred_element_type=f32)
    for p in range(1, n_lhs):
        acc = acc + jnp.dot(a_refs[p][...], wb_refs[p][...], preferred_element_type=f32)
    if has_res:
        acc = acc + extra.pop(0)[...]
    if has_scale:
        acc = acc * extra.pop(0)[...]
    o_ref[...] = acc.astype(o_ref.dtype)


def _matmul(lhs_list, w, res, out_dtype, rows, cols, name, col_scale=None):
    n_lhs = len(lhs_list)
    m, kp = lhs_list[0].shape
    nc = w.shape[1]
    assert w.shape[0] == n_lhs * kp
    rows = min(rows, m)
    cols = min(cols, nc)
    in_specs = [pl.BlockSpec((rows, kp), lambda j, i: (i, 0)) for _ in range(n_lhs)]
    in_specs += [pl.BlockSpec((kp, cols), lambda j, i, p=p: (p, j)) for p in range(n_lhs)]
    args = list(lhs_list) + [w] * n_lhs
    if res is not None:
        in_specs.append(pl.BlockSpec((rows, cols), lambda j, i: (i, j)))
        args.append(res)
    if col_scale is not None:
        in_specs.append(pl.BlockSpec((1, cols), lambda j, i: (0, j)))
        args.append(col_scale.reshape(1, nc).astype(f32))
    return pl.pallas_call(
        functools.partial(_matmul_kernel, n_lhs=n_lhs, has_res=res is not None,
                          has_scale=col_scale is not None),
        grid=(nc // cols, m // rows),
        in_specs=in_specs,
        out_specs=pl.BlockSpec((rows, cols), lambda j, i: (i, j)),
        out_shape=jax.ShapeDtypeStruct((m, nc), out_dtype),
        scratch_shapes=[pltpu.VMEM((kp, cols), bf16) for _ in range(n_lhs)],
        compiler_params=_params("parallel", "arbitrary"),
        name=name,
    )(*args)


def _bf16_part(x):
    bits = lax.bitcast_convert_type(x, jnp.uint32) & jnp.uint32(0xFFFF0000)
    return lax.bitcast_convert_type(bits, f32)


def _lane_tile(x, n):
    return x if n == 1 else jnp.concatenate([x] * n, axis=1)


def _diff_attn_kernel(slopes_ref, q_ref, k_ref, v_ref, lq1_ref, lk1_ref, lq2_ref, lk2_ref, g_ref,
                      o_ref, qa1_ref, qa2_ref, ka1_ref, ka2_ref, acc1_ref, acc2_ref,
                      m1_ref, l1_ref, m2_ref, l2_ref, *, tq, d, lam_init):
    h = pl.program_id(1)
    qi = pl.program_id(2)
    tk = tq
    slope2 = slopes_ref[h]
    lane_q = lax.broadcasted_iota(jnp.int32, (tq, d), 1)
    ones_cols = jnp.where(lane_q < 3, 1.0, 0.0).astype(bf16)
    q = q_ref[...]
    qa1_ref[:, :d] = q[:, :d]
    qa1_ref[:, d:] = ones_cols
    qa2_ref[:, :d] = q[:, d:]
    qa2_ref[:, d:] = ones_cols
    lane_k = lax.broadcasted_iota(jnp.int32, (tk, d), 1)
    v0 = slope2 * lax.broadcasted_iota(jnp.int32, (tk, d), 0).astype(f32)
    hi = _bf16_part(v0)
    r1 = v0 - hi
    lo = _bf16_part(r1)
    lo2 = r1 - lo
    bias_cols = jnp.where(lane_k == 0, hi, jnp.where(lane_k == 1, lo, jnp.where(lane_k == 2, lo2, 0.0)))
    ka1_ref[:, d:] = bias_cols.astype(bf16)
    ka2_ref[:, d:] = bias_cols.astype(bf16)
    rel = (lax.broadcasted_iota(jnp.int32, (tq, tk), 0)
           - lax.broadcasted_iota(jnp.int32, (tq, tk), 1))
    maps = ((qa1_ref, ka1_ref, m1_ref, l1_ref, acc1_ref), (qa2_ref, ka2_ref, m2_ref, l2_ref, acc2_ref))
    for _, _, m_ref, l_ref, acc_ref in maps:
        m_ref[...] = jnp.full(m_ref.shape, NEG_INF, f32)
        l_ref[...] = jnp.zeros(l_ref.shape, f32)
        acc_ref[...] = jnp.zeros(acc_ref.shape, f32)

    def block(j, masked):
        k0 = pl.multiple_of(j * tk, tk)
        kb = k_ref[pl.ds(k0, tk), :]
        vb = v_ref[pl.ds(k0, tk), :]
        cj = slope2 * ((j - qi) * tk).astype(f32)
        for c, (qa_ref, ka_ref, m_ref, l_ref, acc_ref) in enumerate(maps):
            ka_ref[:, :d] = kb[:, c * d:(c + 1) * d]
            s = lax.dot_general(qa_ref[...], ka_ref[...], (((1,), (1,)), ((), ())),
                                preferred_element_type=f32)
            if masked:
                s = jnp.where(rel >= 0, s, NEG_INF)
            m_old = m_ref[...]
            m_new = jnp.maximum(m_old, jnp.max(s, axis=-1, keepdims=True) + cj)
            p = jnp.exp2(s - _lane_tile(m_new - cj, tk // LANES))
            alpha = jnp.exp2(m_old - m_new)
            l_ref[...] = alpha * l_ref[...] + jnp.sum(p, axis=-1, keepdims=True)
            acc_ref[...] = (_lane_tile(alpha, acc_ref.shape[1] // LANES) * acc_ref[...]
                            + jnp.dot(p.astype(bf16), vb, preferred_element_type=f32))
            m_ref[...] = m_new

    def off_diag(j, carry):
        block(j, False)
        return carry

    lax.fori_loop(0, qi, off_diag, 0)
    block(qi, True)

    lam = (jnp.exp(jnp.sum(lq1_ref[...] * lk1_ref[...], axis=-1, keepdims=True))
           - jnp.exp(jnp.sum(lq2_ref[...] * lk2_ref[...], axis=-1, keepdims=True)) + lam_init)
    reps = acc1_ref.shape[1] // LANES
    o = (acc1_ref[...] / _lane_tile(l1_ref[...], reps)
         - lam * (acc2_ref[...] / _lane_tile(l2_ref[...], reps)))
    ms = jnp.mean(o * o, axis=-1, keepdims=True)
    y = o * lax.rsqrt(ms + RMS_EPS) * g_ref[...] * (1.0 - lam_init)
    o_ref[...] = y.astype(o_ref.dtype)


def _diff_attention(proj3, lq1, lk1, lq2, lk2, g_subln, lam_init, cfg, tiles):
    b, s, _ = proj3.shape
    nh = cfg.diff_heads
    d = lq1.shape[-1]
    hw = 2 * d
    tq = min(tiles.attn_q, s)
    slopes = jnp.exp2(-8.0 * jnp.arange(1, nh + 1, dtype=f32) / nh) * LOG2E
    vec = lambda a: a.reshape(1, -1).astype(f32)
    small = lambda n: pl.BlockSpec((1, n), lambda bi, hi, qi: (0, 0))
    return pl.pallas_call(
        functools.partial(_diff_attn_kernel, tq=tq, d=d, lam_init=lam_init),
        grid=(b, nh, s // tq),
        in_specs=[pl.BlockSpec(memory_space=pltpu.SMEM),
                  pl.BlockSpec((None, tq, hw), lambda bi, hi, qi: (bi, qi, hi)),
                  pl.BlockSpec((None, s, hw), lambda bi, hi, qi: (bi, 0, nh + hi)),
                  pl.BlockSpec((None, s, hw), lambda bi, hi, qi: (bi, 0, 2 * nh + hi)),
                  small(d), small(d), small(d), small(d), small(hw)],
        out_specs=pl.BlockSpec((None, tq, hw), lambda bi, hi, qi: (bi, qi, hi)),
        out_shape=jax.ShapeDtypeStruct((b, s, nh * hw), bf16),
        scratch_shapes=[pltpu.VMEM((tq, hw), bf16), pltpu.VMEM((tq, hw), bf16),
                        pltpu.VMEM((tq, hw), bf16), pltpu.VMEM((tq, hw), bf16),
                        pltpu.VMEM((tq, hw), f32), pltpu.VMEM((tq, hw), f32),
                        pltpu.VMEM((tq, LANES), f32), pltpu.VMEM((tq, LANES), f32),
                        pltpu.VMEM((tq, LANES), f32), pltpu.VMEM((tq, LANES), f32)],
        compiler_params=_params("parallel", "parallel", "parallel"),
        name="diff_attention",
    )(slopes, proj3, proj3, proj3, vec(lq1), vec(lk1), vec(lq2), vec(lk2), vec(g_subln))


def _conformer_kernel(a_ref, gate_ref, ah_ref, gateh_ref, wdw_ref, bdw_ref, gln_ref, bln_ref,
                      wpw_ref, o_ref, buf_ref, conv_ref, win_ref, *, ts, kw, halo, rt):
    i = pl.program_id(1)
    ch = a_ref.shape[-1]
    glu = a_ref[...].astype(f32) * jax.nn.sigmoid(gate_ref[...].astype(f32))
    glu_h = ah_ref[...].astype(f32) * jax.nn.sigmoid(gateh_ref[...].astype(f32))
    buf_ref[0:halo, :] = jnp.where(i > 0, glu_h, 0.0)
    buf_ref[halo:halo + ts, :] = glu
    off = halo - (kw - 1)
    def conv_chunk(c, carry):
        cs = pl.ds(pl.multiple_of(c * LANES, LANES), LANES)
        for r0 in range(0, ts, rt):
            acc = jnp.broadcast_to(bdw_ref[:, cs], (rt, LANES))
            for r in range(min(SUBLANES, kw)):
                span = (kw - 1 - r) // SUBLANES * SUBLANES
                win_ref[r, 0:rt + span, :] = buf_ref[r0 + off + r:r0 + off + r + rt + span, cs]
                for j in range(r, kw, SUBLANES):
                    acc = acc + wdw_ref[j:j + 1, cs] * win_ref[r, j - r:j - r + rt, :]
            conv_ref[r0:r0 + rt, cs] = acc
        return carry

    lax.fori_loop(0, ch // LANES, conv_chunk, 0)
    c = conv_ref[...]
    mu = jnp.mean(c, axis=-1, keepdims=True)
    xc = c - mu
    var = jnp.mean(xc * xc, axis=-1, keepdims=True)
    y = xc * lax.rsqrt(var + LN_EPS) * gln_ref[...] + bln_ref[...]
    act = y * jax.nn.sigmoid(y)
    o_ref[...] = jnp.dot(act.astype(bf16), wpw_ref[...],
                         preferred_element_type=f32).astype(o_ref.dtype)


def _conformer(proj3, col0, w_dw, b_dw, g_ln, b_ln, w_pw, cfg, tiles):
    b, s, _ = proj3.shape
    kw = cfg.conv_width
    ch = w_pw.shape[0]
    ts = min(tiles.conv_rows, s)
    halo = tiles.conv_halo
    assert halo >= kw - 1 and ts % halo == 0 and col0 % ch == 0
    cb = col0 // ch
    hb = ts // halo
    rt = math.gcd(ts, tiles.conv_acc_rows)
    win_rows = rt + (kw - 1) // SUBLANES * SUBLANES
    vec = lambda a: a.reshape(1, ch).astype(f32)
    small = lambda: pl.BlockSpec((1, ch), lambda bi, i: (0, 0))
    return pl.pallas_call(
        functools.partial(_conformer_kernel, ts=ts, kw=kw, halo=halo, rt=rt),
        grid=(b, s // ts),
        in_specs=[pl.BlockSpec((None, ts, ch), lambda bi, i: (bi, i, cb)),
                  pl.BlockSpec((None, ts, ch), lambda bi, i: (bi, i, cb + 1)),
                  pl.BlockSpec((None, halo, ch), lambda bi, i: (bi, jnp.maximum(i * hb - 1, 0), cb)),
                  pl.BlockSpec((None, halo, ch), lambda bi, i: (bi, jnp.maximum(i * hb - 1, 0), cb + 1)),
                  pl.BlockSpec((kw, ch), lambda bi, i: (0, 0)),
                  small(), small(), small(),
                  pl.BlockSpec((ch, ch), lambda bi, i: (0, 0))],
        out_specs=pl.BlockSpec((None, ts, ch), lambda bi, i: (bi, i, 0)),
        out_shape=jax.ShapeDtypeStruct((b, s, ch), bf16),
        scratch_shapes=[pltpu.VMEM((halo + ts, ch), f32), pltpu.VMEM((ts, ch), f32),
                        pltpu.VMEM((SUBLANES, win_rows, LANES), f32)],
        compiler_params=_params("parallel", "parallel"),
        name="conformer",
    )(proj3, proj3, proj3, proj3, w_dw.reshape(kw, ch).astype(f32), vec(b_dw), vec(g_ln), vec(b_ln),
      w_pw)


def _xattn_kernel(q_ref, k_ref, v_ref, o_ref, *, scale):
    s = lax.dot_general(q_ref[...], k_ref[...], (((1,), (1,)), ((), ())),
                        preferred_element_type=f32) * scale
    m = jnp.max(s, axis=-1, keepdims=True)
    p = jnp.exp(s - m)
    p = p / jnp.sum(p, axis=-1, keepdims=True)
    o_ref[...] = jnp.dot(p.astype(bf16), v_ref[...], preferred_element_type=f32).astype(o_ref.dtype)


def _cross_attention(q3, k3, v3, cfg, tiles):
    b, s, dm = q3.shape
    m = k3.shape[1]
    nh = cfg.x_heads
    hd = dm // nh
    tq = min(tiles.xattn_q, s)
    return pl.pallas_call(
        functools.partial(_xattn_kernel, scale=hd ** -0.5),
        grid=(b, s // tq, nh),
        in_specs=[pl.BlockSpec((None, tq, hd), lambda bi, i, h: (bi, i, h)),
                  pl.BlockSpec((None, m, hd), lambda bi, i, h: (bi, 0, h)),
                  pl.BlockSpec((None, m, hd), lambda bi, i, h: (bi, 0, h))],
        out_specs=pl.BlockSpec((None, tq, hd), lambda bi, i, h: (bi, i, h)),
        out_shape=jax.ShapeDtypeStruct((b, s, dm), bf16),
        compiler_params=_params("parallel", "parallel", "parallel"),
        name="cross_attention",
    )(q3, k3, v3)


def _store_token_rows(o_ref, y, n_rows, n_sub):
    _store_token_chunks(o_ref, y, n_rows, n_sub, 0)
    o_ref[pl.ds(n_sub, n_rows, stride=n_sub + SLAB_PAD), :] = jnp.zeros((n_rows, LANES), o_ref.dtype)


def _store_token_chunks(o_ref, y, n_rows, n_sub, s0):
    pitch = n_sub + SLAB_PAD
    for s in range(y.shape[1] // LANES):
        o_ref[pl.ds(s0 + s, n_rows, stride=pitch), :] = y[:, s * LANES:(s + 1) * LANES]


def _router_kernel(x_ref, g_ref, wr_ref, br_ref, hg_ref, te_ref, tw_ref, cnt_ref, *, tm, n_sub, top_k):
    x = x_ref[...]
    ms = jnp.mean(x * x, axis=-1, keepdims=True)
    h = x * lax.rsqrt(ms + RMS_EPS) * g_ref[...]
    _store_token_rows(hg_ref, h, tm, n_sub)
    logits = jnp.dot(h.astype(bf16), wr_ref[...], preferred_element_type=f32) + br_ref[...]
    lane = lax.broadcasted_iota(jnp.int32, logits.shape, 1)
    lane_f = lane.astype(f32)
    vals, idxs = [], []
    for _ in range(top_k):
        m = jnp.max(logits, axis=-1, keepdims=True)
        idx = jnp.min(jnp.where(logits == m, lane_f, float(LANES)), axis=-1, keepdims=True)
        vals.append(m)
        idxs.append(idx)
        logits = jnp.where(lane_f == idx, NEG_INF * 2.0, logits)
    exps = [jnp.exp(v - vals[0]) for v in vals]
    denom = exps[0]
    for e in exps[1:]:
        denom = denom + e
    te = jnp.zeros(logits.shape, f32)
    tw = jnp.zeros(logits.shape, f32)
    hits = jnp.zeros(logits.shape, f32)
    for k in range(top_k):
        te = jnp.where(lane == k, idxs[k], te)
        tw = jnp.where(lane == k, exps[k] / denom, tw)
        hits = hits + jnp.where(lane_f == idxs[k], 1.0, 0.0)
    te_ref[...] = te.astype(jnp.int32)
    tw_ref[...] = tw
    tile_counts = jnp.sum(hits, axis=0, keepdims=True)
    row8 = lax.broadcasted_iota(jnp.int32, cnt_ref.shape, 0)
    cnt_ref[...] = jnp.where(row8 == 0, tile_counts, 0.0).astype(jnp.int32)


def _norm_router(x2d, g, w_router, b_router, cfg, tiles):
    n, d = x2d.shape
    e = w_router.shape[1]
    assert e <= LANES and d % LANES == 0
    n_sub = d // LANES
    pitch = n_sub + SLAB_PAD
    tm = min(tiles.router_rows, n)
    wr = jnp.zeros((d, LANES), bf16).at[:, :e].set(w_router.astype(bf16))
    br = jnp.full((1, LANES), NEG_INF, f32).at[0, :e].set(b_router.astype(f32))
    hg, te, tw, cnt = pl.pallas_call(
        functools.partial(_router_kernel, tm=tm, n_sub=n_sub, top_k=cfg.top_k),
        grid=(n // tm,),
        in_specs=[pl.BlockSpec((tm, d), lambda i: (i, 0)),
                  pl.BlockSpec((1, d), lambda i: (0, 0)),
                  pl.BlockSpec((d, LANES), lambda i: (0, 0)),
                  pl.BlockSpec((1, LANES), lambda i: (0, 0))],
        out_specs=[pl.BlockSpec((tm * pitch, LANES), lambda i: (i, 0)),
                   pl.BlockSpec((tm, LANES), lambda i: (i, 0)),
                   pl.BlockSpec((tm, LANES), lambda i: (i, 0)),
                   pl.BlockSpec((SUBLANES, LANES), lambda i: (i, 0))],
        out_shape=[jax.ShapeDtypeStruct((n * pitch, LANES), f32),
                   jax.ShapeDtypeStruct((n, LANES), jnp.int32),
                   jax.ShapeDtypeStruct((n, LANES), f32),
                   jax.ShapeDtypeStruct((n // tm * SUBLANES, LANES), jnp.int32)],
        compiler_params=_params("parallel"),
        name="norm_router",
    )(x2d, g.reshape(1, d).astype(f32), wr, br)
    counts = jnp.sum(cnt, axis=0)[:e]
    return hg, te[:, :cfg.top_k], tw[:, :cfg.top_k], counts


def _routing_plan(top_e, gate_w, counts, blk):
    i32 = jnp.int32
    n_tok, top_k = top_e.shape
    n_experts = counts.shape[0]
    n_slot = n_tok * top_k
    slot_e = top_e.reshape(n_slot)
    slot_w = gate_w.reshape(n_slot)
    order = jnp.argsort(slot_e, stable=True).astype(i32)
    inv = jnp.argsort(order).astype(i32)
    experts = jnp.arange(n_experts, dtype=i32)
    padded = (counts + blk - 1) // blk * blk
    pad_end = jnp.cumsum(padded)
    pad_start = pad_end - padded
    start = jnp.cumsum(counts) - counts
    shift = pad_start - start
    n_blocks = -(-n_slot // blk) + n_experts
    n_rows = n_blocks * blk
    n_used = pad_end[-1] // blk
    last_e = jnp.max(jnp.where(counts > 0, experts, 0))
    block_start = jnp.arange(n_blocks, dtype=i32) * blk
    block_e = jnp.minimum(jnp.sum((pad_end[None, :] <= block_start[:, None]).astype(i32), axis=1), last_e)
    row = (block_start[:, None] + jnp.arange(blk, dtype=i32)[None, :])
    in_expert = row - pad_start[block_e][:, None]
    valid = (in_expert < counts[block_e][:, None]) & (row < pad_end[-1])
    row_slot = order[jnp.where(valid, row - shift[block_e][:, None], 0).reshape(n_rows)]
    valid = valid.reshape(n_rows)
    row_tok = jnp.where(valid, row_slot // top_k, 0)
    row_w = jnp.where(valid, slot_w[row_slot], 0.0)
    pos = inv + shift[slot_e]
    live = jnp.where(counts > 0, experts, n_experts)
    later = jnp.concatenate([lax.cummin(live[::-1])[::-1][1:], jnp.full((1,), n_experts, i32)])
    next_live = jnp.where(later < n_experts, later, -1)
    prev_e = jnp.concatenate([jnp.full((1,), -1, i32), block_e[:-1]])
    run_first = ((block_e != prev_e) & (block_start < pad_end[-1])).astype(i32)
    misc = jnp.stack([n_used, jnp.min(live)]).astype(i32)
    return row_tok, row_w.reshape(n_rows, 1), (block_e, run_first, next_live[block_e], misc), pos


def _gather_kernel(tok_ref, nu_ref, hg_hbm, o_ref, x3a_ref, x3b_ref, sem, *, blk, n_sub, unroll):
    b = pl.program_id(0)
    n_used = nu_ref[0]
    pitch = n_sub + SLAB_PAD
    bufs = (x3a_ref, x3b_ref)

    def issue(block_idx, slot):
        def body(r0, carry):
            for q in range(unroll):
                r = r0 * unroll + q
                tok = tok_ref[block_idx * blk + r]
                pltpu.make_async_copy(hg_hbm.at[pl.ds(tok * pitch, n_sub)], bufs[slot].at[:, r, :],
                                      sem.at[slot]).start()
            return carry
        lax.fori_loop(0, blk // unroll, body, 0)

    def wait_all(slot):
        pltpu.make_async_copy(bufs[slot], bufs[slot], sem.at[slot]).wait()

    @pl.when(b < n_used)
    def _():
        @pl.when(b == 0)
        def _():
            issue(0, 0)
        for slot in range(2):
            @pl.when(b % 2 == slot)
            def _():
                @pl.when(b + 1 < n_used)
                def _():
                    issue(b + 1, 1 - slot)
                wait_all(slot)
                for s in range(n_sub):
                    o_ref[:, s * LANES:(s + 1) * LANES] = bufs[slot][s].astype(o_ref.dtype)

    @pl.when(b >= n_used)
    def _():
        o_ref[...] = jnp.zeros(o_ref.shape, o_ref.dtype)


def _stream_expert_weights(b, f, nf, be_ref, first_ref, nxt_ref, first_e, copies, stages, works):
    @pl.when(first_ref[b] == 1)
    def _():
        e = be_ref[b]

        @pl.when((f == 0) & (b == 0))
        def _():
            for copy in copies:
                copy(e, f).start()
        for copy in copies:
            copy(e, f).wait()
        for stage, work in zip(stages, works):
            work[...] = stage[...].astype(work.dtype)
        more = nxt_ref[b] >= 0

        @pl.when(more | (f + 1 < nf))
        def _():
            e2 = jnp.where(more, nxt_ref[b], first_e)
            f2 = jnp.where(more, f, f + 1)
            for copy in copies:
                copy(e2, f2).start()


def _gate_up_kernel(be_ref, first_ref, nxt_ref, misc_ref, x_ref, wg_hbm, bg_ref, wu_hbm, bu_ref, o_ref,
                    sg_ref, su_ref, wg_ref, wu_ref, sem, *, tf, nf):
    f = pl.program_id(0)
    b = pl.program_id(1)
    n_used, first_e = misc_ref[0], misc_ref[1]

    def tile_copy(hbm, stage, k):
        return lambda e, fi: pltpu.make_async_copy(
            hbm.at[e, :, pl.ds(pl.multiple_of(fi * tf, tf), tf)], stage, sem.at[k])

    @pl.when(b < n_used)
    def _():
        _stream_expert_weights(b, f, nf, be_ref, first_ref, nxt_ref, first_e,
                               (tile_copy(wg_hbm, sg_ref, 0), tile_copy(wu_hbm, su_ref, 1)),
                               (sg_ref, su_ref), (wg_ref, wu_ref))
        x = x_ref[...]
        g = jnp.minimum(jnp.dot(x, wg_ref[...], preferred_element_type=f32) + bg_ref[...], SWIGLU_LIMIT)
        up = jnp.clip(jnp.dot(x, wu_ref[...], preferred_element_type=f32) + bu_ref[...],
                      -SWIGLU_LIMIT, SWIGLU_LIMIT)
        o_ref[...] = ((up + 1.0) * (g * jax.nn.sigmoid(SWIGLU_ALPHA * g))).astype(o_ref.dtype)

    @pl.when(b >= n_used)
    def _():
        o_ref[...] = jnp.zeros(o_ref.shape, o_ref.dtype)


def _down_kernel(be_ref, first_ref, nxt_ref, misc_ref, a_ref, wd_hbm, bd_ref, rw_ref, o_ref,
                 sd_ref, wd_ref, sem, *, blk, n_sub, tc):
    b = pl.program_id(0)
    d = n_sub * LANES
    n_used, first_e = misc_ref[0], misc_ref[1]

    @pl.when(b < n_used)
    def _():
        copy = lambda e, fi: pltpu.make_async_copy(wd_hbm.at[e], sd_ref, sem.at[0])
        _stream_expert_weights(b, 0, 1, be_ref, first_ref, nxt_ref, first_e, (copy,), (sd_ref,), (wd_ref,))
        a = a_ref[...]
        rw = rw_ref[...]
        for c in range(d // tc):
            cs = slice(c * tc, (c + 1) * tc)
            y = jnp.dot(a, wd_ref[:, cs], preferred_element_type=f32) + bd_ref[:, cs]
            _store_token_chunks(o_ref, y * rw, blk, n_sub, c * tc // LANES)
        o_ref[pl.ds(n_sub, blk, stride=n_sub + SLAB_PAD), :] = jnp.zeros((blk, LANES), o_ref.dtype)

    @pl.when(b >= n_used)
    def _():
        o_ref[...] = jnp.zeros(o_ref.shape, o_ref.dtype)


def _experts(hg, row_tok, row_w, runs, wg, bg, wu, bu, wd, bd, tiles):
    block_e, run_first, run_next, misc = runs
    n_e, d, ff = wg.shape
    n_sub = d // LANES
    pitch = n_sub + SLAB_PAD
    blk = tiles.expert_rows
    n_rows = row_tok.shape[0]
    n_blocks = n_rows // blk
    tf = min(tiles.expert_cols, ff)
    nf = ff // tf
    xs = pl.pallas_call(
        functools.partial(_gather_kernel, blk=blk, n_sub=n_sub, unroll=math.gcd(blk, tiles.gather_unroll)),
        grid_spec=pltpu.PrefetchScalarGridSpec(
            num_scalar_prefetch=2,
            grid=(n_blocks,),
            in_specs=[pl.BlockSpec(memory_space=pl.ANY)],
            out_specs=pl.BlockSpec((blk, d), lambda b, tok, nu: (b, 0)),
            scratch_shapes=[pltpu.VMEM((n_sub, blk, LANES), f32), pltpu.VMEM((n_sub, blk, LANES), f32),
                            pltpu.SemaphoreType.DMA((2,))]),
        out_shape=jax.ShapeDtypeStruct((n_rows, d), bf16),
        compiler_params=_params("arbitrary"),
        name="expert_gather",
    )(row_tok, misc, hg)
    xb = lambda f, b, be, fi, nx, ms: (jnp.minimum(b, ms[0] - 1), 0)
    bias = lambda f, b, be, fi, nx, ms: (be[b], 0, f)
    act = pl.pallas_call(
        functools.partial(_gate_up_kernel, tf=tf, nf=nf),
        grid_spec=pltpu.PrefetchScalarGridSpec(
            num_scalar_prefetch=4,
            grid=(nf, n_blocks),
            in_specs=[pl.BlockSpec((blk, d), xb),
                      pl.BlockSpec(memory_space=pl.ANY),
                      pl.BlockSpec((None, 1, tf), bias),
                      pl.BlockSpec(memory_space=pl.ANY),
                      pl.BlockSpec((None, 1, tf), bias)],
            out_specs=pl.BlockSpec((blk, tf), lambda f, b, be, fi, nx, ms: (b, f)),
            scratch_shapes=[pltpu.VMEM((d, tf), f32), pltpu.VMEM((d, tf), f32),
                            pltpu.VMEM((d, tf), bf16), pltpu.VMEM((d, tf), bf16),
                            pltpu.SemaphoreType.DMA((2,))]),
        out_shape=jax.ShapeDtypeStruct((n_rows, ff), bf16),
        compiler_params=_params("arbitrary", "arbitrary"),
        name="expert_gate_up",
    )(block_e, run_first, run_next, misc, xs, wg, bg.reshape(n_e, 1, ff).astype(f32), wu,
      bu.reshape(n_e, 1, ff).astype(f32))
    tc = min(tiles.down_cols, d)
    return pl.pallas_call(
        functools.partial(_down_kernel, blk=blk, n_sub=n_sub, tc=tc),
        grid_spec=pltpu.PrefetchScalarGridSpec(
            num_scalar_prefetch=4,
            grid=(n_blocks,),
            in_specs=[pl.BlockSpec((blk, ff), lambda b, be, fi, nx, ms: (jnp.minimum(b, ms[0] - 1), 0)),
                      pl.BlockSpec(memory_space=pl.ANY),
                      pl.BlockSpec((None, 1, d), lambda b, be, fi, nx, ms: (be[b], 0, 0)),
                      pl.BlockSpec((blk, 1), lambda b, be, fi, nx, ms: (b, 0))],
            out_specs=pl.BlockSpec((blk * pitch, LANES), lambda b, be, fi, nx, ms: (b, 0)),
            scratch_shapes=[pltpu.VMEM((ff, d), f32), pltpu.VMEM((ff, d), bf16),
                            pltpu.SemaphoreType.DMA((1,))]),
        out_shape=jax.ShapeDtypeStruct((n_rows * pitch, LANES), f32),
        compiler_params=_params("arbitrary"),
        name="expert_down",
    )(block_e, run_first, run_next, misc, act, wd, bd.reshape(n_e, 1, d).astype(f32), row_w)


def _combine_kernel(pos_ref, x_ref, ys_hbm, g_ref, o_ref, b0_ref, b1_ref, sem, *, tm, n_sub, top_k,
                    final_norm):
    i = pl.program_id(0)
    nb = pl.num_programs(0)
    bufs = (b0_ref, b1_ref)
    pitch = n_sub + SLAB_PAD

    def issue(block_idx, slot):
        def body(t0, carry):
            for q in range(2):
                t = t0 * 2 + q
                for k in range(top_k):
                    row = pos_ref[(block_idx * tm + t) * top_k + k]
                    pltpu.make_async_copy(ys_hbm.at[pl.ds(row * pitch, n_sub)],
                                          bufs[slot].at[:, k * tm + t, :], sem.at[slot]).start()
            return carry
        lax.fori_loop(0, tm // 2, body, 0)

    def wait_all(slot):
        pltpu.make_async_copy(bufs[slot], bufs[slot], sem.at[slot]).wait()

    @pl.when(i == 0)
    def _():
        issue(0, 0)

    for slot in range(2):
        @pl.when(i % 2 == slot)
        def _():
            @pl.when(i + 1 < nb)
            def _():
                issue(i + 1, 1 - slot)
            wait_all(slot)
            for s in range(n_sub):
                cs = slice(s * LANES, (s + 1) * LANES)
                acc = x_ref[:, cs]
                for k in range(top_k):
                    acc = acc + bufs[slot][s, k * tm:(k + 1) * tm, :]
                o_ref[:, cs] = acc

    if final_norm:
        x = o_ref[...]
        ms = jnp.mean(x * x, axis=-1, keepdims=True)
        o_ref[...] = x * lax.rsqrt(ms + RMS_EPS) * g_ref[...]


def _combine(x2d, ys, pos, g_final, final_norm, cfg, tiles):
    n, d = x2d.shape
    n_sub = d // LANES
    tm = min(tiles.combine_rows, n)
    top_k = cfg.top_k
    grid_spec = pltpu.PrefetchScalarGridSpec(
        num_scalar_prefetch=1,
        grid=(n // tm,),
        in_specs=[pl.BlockSpec((tm, d), lambda i, pos: (i, 0)),
                  pl.BlockSpec(memory_space=pl.ANY),
                  pl.BlockSpec((1, d), lambda i, pos: (0, 0))],
        out_specs=pl.BlockSpec((tm, d), lambda i, pos: (i, 0)),
        scratch_shapes=[pltpu.VMEM((n_sub, top_k * tm, LANES), f32),
                        pltpu.VMEM((n_sub, top_k * tm, LANES), f32),
                        pltpu.SemaphoreType.DMA((2,))],
    )
    return pl.pallas_call(
        functools.partial(_combine_kernel, tm=tm, n_sub=n_sub, top_k=top_k, final_norm=final_norm),
        grid_spec=grid_spec,
        out_shape=jax.ShapeDtypeStruct((n, d), f32),
        compiler_params=_params("arbitrary"),
        name="moe_combine",
    )(pos, x2d, ys, g_final.reshape(1, d).astype(f32))


def _layer(x, mem, p, l, is_last, g_final, cfg, tiles):
    b, s, d = x.shape
    n = b * s
    m_len = mem.shape[1]
    diff_width = d // 2
    conv_ch = d - diff_width
    x2d = x.reshape(n, d)

    h = _rmsnorm(x2d, p['g_mix'][l], tiles.norm_rows)
    head_dim = p['lambda_q1'].shape[-1]
    in_cols = p['w_in'].shape[-1]
    q_scale = jnp.where(jnp.arange(in_cols) < diff_width, head_dim ** -0.5 * LOG2E, 1.0)
    proj = _matmul([h], p['w_in'][l], None, bf16, tiles.mm_rows, tiles.mm_cols, "in_proj",
                   col_scale=q_scale)
    proj3 = proj.reshape(b, s, -1)
    lam_init = 0.8 - 0.6 * math.exp(-0.3 * l)
    a_out = _diff_attention(proj3, p['lambda_q1'][l], p['lambda_k1'][l], p['lambda_q2'][l],
                            p['lambda_k2'][l], p['g_subln'][l], lam_init, cfg, tiles)
    c_out = _conformer(proj3, 3 * diff_width, p['w_dw'][l], p['b_dw'][l], p['g_conv_ln'][l],
                       p['b_conv_ln'][l], p['w_conv_pw'][l].astype(bf16), cfg, tiles)
    assert diff_width == conv_ch
    x2d = _matmul([a_out.reshape(n, diff_width), c_out.reshape(n, conv_ch)],
                  p['w_out'][l], x2d, f32, tiles.mm_rows, tiles.mm_cols, "out_proj")

    hc = _rmsnorm(x2d, p['g_cross'][l], tiles.norm_rows)
    mem_n = _rmsnorm(mem.reshape(b * m_len, d), p['g_mem'][l], tiles.norm_rows)
    qc = _matmul([hc], p['w_cq'][l], None, bf16, tiles.mm_rows, tiles.mm_cols, "xq_proj")
    kc = _matmul([mem_n], p['w_ck'][l], None, bf16, tiles.mm_rows, tiles.mm_cols, "xk_proj")
    vc = _matmul([mem_n], p['w_cv'][l], None, bf16, tiles.mm_rows, tiles.mm_cols, "xv_proj")
    oc = _cross_attention(qc.reshape(b, s, d), kc.reshape(b, m_len, d), vc.reshape(b, m_len, d),
                          cfg, tiles)
    x2d = _matmul([oc.reshape(n, d)], p['w_co'][l], x2d, f32, tiles.mm_rows, tiles.mm_cols, "xo_proj")

    hg, top_e, gate_w, counts = _norm_router(x2d, p['g_ffn'][l], p['w_router'][l], p['b_router'][l],
                                             cfg, tiles)
    row_tok, row_w, runs, pos = _routing_plan(top_e, gate_w, counts, tiles.expert_rows)
    ys = _experts(hg, row_tok, row_w, runs, p['w_gate'][l], p['b_gate'][l], p['w_up'][l], p['b_up'][l],
                  p['w_down'][l], p['b_down'][l], tiles)
    x2d = _combine(x2d, ys, pos, g_final, is_last, cfg, tiles)
    return x2d.reshape(b, s, d)


def _forward(x, mem, params, g_final, cfg, tiles):
    depth = params['g_mix'].shape[0]
    for l in range(depth):
        x = _layer(x, mem, params, l, l == depth - 1, g_final, cfg, tiles)
    return x


def kernel(x, mem, g_mix, w_in, lambda_q1, lambda_k1, lambda_q2, lambda_k2, g_subln, w_dw, b_dw,
           g_conv_ln, b_conv_ln, w_conv_pw, w_out, g_cross, g_mem, w_cq, w_ck, w_cv, w_co, g_ffn,
           w_router, b_router, w_gate, b_gate, w_up, b_up, w_down, b_down, g_final):
    params = dict(g_mix=g_mix, w_in=w_in, lambda_q1=lambda_q1, lambda_k1=lambda_k1,
                  lambda_q2=lambda_q2, lambda_k2=lambda_k2, g_subln=g_subln, w_dw=w_dw, b_dw=b_dw,
                  g_conv_ln=g_conv_ln, b_conv_ln=b_conv_ln, w_conv_pw=w_conv_pw, w_out=w_out,
                  g_cross=g_cross, g_mem=g_mem, w_cq=w_cq, w_ck=w_ck, w_cv=w_cv, w_co=w_co,
                  g_ffn=g_ffn, w_router=w_router, b_router=b_router, w_gate=w_gate, b_gate=b_gate,
                  w_up=w_up, b_up=b_up, w_down=w_down, b_down=b_down)
    return _forward(x, mem, params, g_final, Cfg(), Tiles())
```

```python
import functools
import math
from typing import NamedTuple

import jax
import jax.numpy as jnp
from jax import lax
from jax.experimental import pallas as pl
from jax.experimental.pallas import tpu as pltpu

f32 = jnp.float32
bf16 = jnp.bfloat16

RMS_EPS = 1e-6
LN_EPS = 1e-5
NEG_INF = -1e30
SWIGLU_LIMIT = 7.0
SWIGLU_ALPHA = 1.702
LOG2E = 1.4426950408889634

LANES = 128
SUBLANES = 8
SLAB_PAD = 1
VMEM_LIMIT_BYTES = 56 * 1024 * 1024


class Cfg(NamedTuple):
    diff_heads: int = 8
    conv_width: int = 31
    x_heads: int = 4
    top_k: int = 4


class Tiles(NamedTuple):
    norm_rows: int = 256
    mm_rows: int = 1024
    mm_cols: int = 512
    attn_q: int = 1024
    conv_rows: int = 256
    conv_halo: int = 32
    conv_acc_rows: int = 128
    xattn_q: int = 512
    router_rows: int = 256
    expert_rows: int = 256
    gather_unroll: int = 8
    expert_cols: int = 768
    down_cols: int = 512
    combine_rows: int = 128


def _params(*sem):
    return pltpu.CompilerParams(dimension_semantics=sem, vmem_limit_bytes=VMEM_LIMIT_BYTES)


def _rmsnorm_kernel(x_ref, g_ref, o_ref):
    x = x_ref[...]
    ms = jnp.mean(x * x, axis=-1, keepdims=True)
    o_ref[...] = (x * lax.rsqrt(ms + RMS_EPS) * g_ref[...]).astype(o_ref.dtype)


def _rmsnorm(x2d, g, rows):
    n, d = x2d.shape
    rows = min(rows, n)
    return pl.pallas_call(
        _rmsnorm_kernel,
        grid=(n // rows,),
        in_specs=[pl.BlockSpec((rows, d), lambda i: (i, 0)),
                  pl.BlockSpec((1, d), lambda i: (0, 0))],
        out_specs=pl.BlockSpec((rows, d), lambda i: (i, 0)),
        out_shape=jax.ShapeDtypeStruct((n, d), bf16),
        compiler_params=_params("parallel"),
        name="rmsnorm",
    )(x2d, g.reshape(1, d).astype(f32))


def _matmul_kernel(*refs, n_lhs, has_res, has_scale):
    a_refs = refs[:n_lhs]
    w_refs = refs[n_lhs:2 * n_lhs]
    extra = list(refs[2 * n_lhs:-1 - n_lhs])
    o_ref = refs[-1 - n_lhs]
    wb_refs = refs[len(refs) - n_lhs:]

    @pl.when(pl.program_id(1) == 0)
    def _():
        for w_ref, wb_ref in zip(w_refs, wb_refs):
            wb_ref[...] = w_ref[...].astype(bf16)

    acc = jnp.dot(a_refs[0][...], wb_refs[0][...], preferred_element_type=f32)
    for p in range(1, n_lhs):
        acc = acc + jnp.dot(a_refs[p][...], wb_refs[p][...], preferred_element_type=f32)
    if has_res:
        acc = acc + extra.pop(0)[...]
    if has_scale:
        acc = acc * extra.pop(0)[...]
    o_ref[...] = acc.astype(o_ref.dtype)


def _matmul(lhs_list, w, res, out_dtype, rows, cols, name, col_scale=None):
    n_lhs = len(lhs_list)
    m, kp = lhs_list[0].shape
    nc = w.shape[1]
    assert w.shape[0] == n_lhs * kp
    rows = min(rows, m)
    cols = min(cols, nc)
    in_specs = [pl.BlockSpec((rows, kp), lambda j, i: (i, 0)) for _ in range(n_lhs)]
    in_specs += [pl.BlockSpec((kp, cols), lambda j, i, p=p: (p, j)) for p in range(n_lhs)]
    args = list(lhs_list) + [w] * n_lhs
    if res is not None:
        in_specs.append(pl.BlockSpec((rows, cols), lambda j, i: (i, j)))
        args.append(res)
    if col_scale is not None:
        in_specs.append(pl.BlockSpec((1, cols), lambda j, i: (0, j)))
        args.append(col_scale.reshape(1, nc).astype(f32))
    return pl.pallas_call(
        functools.partial(_matmul_kernel, n_lhs=n_lhs, has_res=res is not None,
                          has_scale=col_scale is not None),
        grid=(nc // cols, m // rows),
        in_specs=in_specs,
        out_specs=pl.BlockSpec((rows, cols), lambda j, i: (i, j)),
        out_shape=jax.ShapeDtypeStruct((m, nc), out_dtype),
        scratch_shapes=[pltpu.VMEM((kp, cols), bf16) for _ in range(n_lhs)],
        compiler_params=_params("parallel", "arbitrary"),
        name=name,
    )(*args)


def _bf16_part(x):
    bits = lax.bitcast_convert_type(x, jnp.uint32) & jnp.uint32(0xFFFF0000)
    return lax.bitcast_convert_type(bits, f32)


def _lane_tile(x, n):
    return x if n == 1 else jnp.concatenate([x] * n, axis=1)


def _diff_attn_kernel(slopes_ref, q_ref, k_ref, v_ref, lq1_ref, lk1_ref, lq2_ref, lk2_ref, g_ref,
                      o_ref, qa1_ref, qa2_ref, ka1_ref, ka2_ref, acc1_ref, acc2_ref,
                      m1_ref, l1_ref, m2_ref, l2_ref, *, tq, d, lam_init):
    h = pl.program_id(1)
    qi = pl.program_id(2)
    tk = tq
    slope2 = slopes_ref[h]
    lane_q = lax.broadcasted_iota(jnp.int32, (tq, d), 1)
    ones_cols = jnp.where(lane_q < 3, 1.0, 0.0).astype(bf16)
    q = q_ref[...]
    qa1_ref[:, :d] = q[:, :d]
    qa1_ref[:, d:] = ones_cols
    qa2_ref[:, :d] = q[:, d:]
    qa2_ref[:, d:] = ones_cols
    lane_k = lax.broadcasted_iota(jnp.int32, (tk, d), 1)
    v0 = slope2 * lax.broadcasted_iota(jnp.int32, (tk, d), 0).astype(f32)
    hi = _bf16_part(v0)
    r1 = v0 - hi
    lo = _bf16_part(r1)
    lo2 = r1 - lo
    bias_cols = jnp.where(lane_k == 0, hi, jnp.where(lane_k == 1, lo, jnp.where(lane_k == 2, lo2, 0.0)))
    ka1_ref[:, d:] = bias_cols.astype(bf16)
    ka2_ref[:, d:] = bias_cols.astype(bf16)
    rel = (lax.broadcasted_iota(jnp.int32, (tq, tk), 0)
           - lax.broadcasted_iota(jnp.int32, (tq, tk), 1))
    maps = ((qa1_ref, ka1_ref, m1_ref, l1_ref, acc1_ref), (qa2_ref, ka2_ref, m2_ref, l2_ref, acc2_ref))
    for _, _, m_ref, l_ref, acc_ref in maps:
        m_ref[...] = jnp.full(m_ref.shape, NEG_INF, f32)
        l_ref[...] = jnp.zeros(l_ref.shape, f32)
        acc_ref[...] = jnp.zeros(acc_ref.shape, f32)

    def block(j, masked):
        k0 = pl.multiple_of(j * tk, tk)
        kb = k_ref[pl.ds(k0, tk), :]
        vb = v_ref[pl.ds(k0, tk), :]
        cj = slope2 * ((j - qi) * tk).astype(f32)
        for c, (qa_ref, ka_ref, m_ref, l_ref, acc_ref) in enumerate(maps):
            ka_ref[:, :d] = kb[:, c * d:(c + 1) * d]
            s = lax.dot_general(qa_ref[...], ka_ref[...], (((1,), (1,)), ((), ())),
                                preferred_element_type=f32)
            if masked:
                s = jnp.where(rel >= 0, s, NEG_INF)
            m_old = m_ref[...]
            m_new = jnp.maximum(m_old, jnp.max(s, axis=-1, keepdims=True) + cj)
            p = jnp.exp2(s - _lane_tile(m_new - cj, tk // LANES))
            alpha = jnp.exp2(m_old - m_new)
            l_ref[...] = alpha * l_ref[...] + jnp.sum(p, axis=-1, keepdims=True)
            acc_ref[...] = (_lane_tile(alpha, acc_ref.shape[1] // LANES) * acc_ref[...]
                            + jnp.dot(p.astype(bf16), vb, preferred_element_type=f32))
            m_ref[...] = m_new

    def off_diag(j, carry):
        block(j, False)
        return carry

    lax.fori_loop(0, qi, off_diag, 0)
    block(qi, True)

    lam = (jnp.exp(jnp.sum(lq1_ref[...] * lk1_ref[...], axis=-1, keepdims=True))
           - jnp.exp(jnp.sum(lq2_ref[...] * lk2_ref[...], axis=-1, keepdims=True)) + lam_init)
    reps = acc1_ref.shape[1] // LANES
    o = (acc1_ref[...] / _lane_tile(l1_ref[...], reps)
         - lam * (acc2_ref[...] / _lane_tile(l2_ref[...], reps)))
    ms = jnp.mean(o * o, axis=-1, keepdims=True)
    y = o * lax.rsqrt(ms + RMS_EPS) * g_ref[...] * (1.0 - lam_init)
    o_ref[...] = y.astype(o_ref.dtype)


def _diff_attention(proj3, lq1, lk1, lq2, lk2, g_subln, lam_init, cfg, tiles):
    b, s, _ = proj3.shape
    nh = cfg.diff_heads
    d = lq1.shape[-1]
    hw = 2 * d
    tq = min(tiles.attn_q, s)
    slopes = jnp.exp2(-8.0 * jnp.arange(1, nh + 1, dtype=f32) / nh) * LOG2E
    vec = lambda a: a.reshape(1, -1).astype(f32)
    small = lambda n: pl.BlockSpec((1, n), lambda bi, hi, qi: (0, 0))
    return pl.pallas_call(
        functools.partial(_diff_attn_kernel, tq=tq, d=d, lam_init=lam_init),
        grid=(b, nh, s // tq),
        in_specs=[pl.BlockSpec(memory_space=pltpu.SMEM),
                  pl.BlockSpec((None, tq, hw), lambda bi, hi, qi: (bi, qi, hi)),
                  pl.BlockSpec((None, s, hw), lambda bi, hi, qi: (bi, 0, nh + hi)),
                  pl.BlockSpec((None, s, hw), lambda bi, hi, qi: (bi, 0, 2 * nh + hi)),
                  small(d), small(d), small(d), small(d), small(hw)],
        out_specs=pl.BlockSpec((None, tq, hw), lambda bi, hi, qi: (bi, qi, hi)),
        out_shape=jax.ShapeDtypeStruct((b, s, nh * hw), bf16),
        scratch_shapes=[pltpu.VMEM((tq, hw), bf16), pltpu.VMEM((tq, hw), bf16),
                        pltpu.VMEM((tq, hw), bf16), pltpu.VMEM((tq, hw), bf16),
                        pltpu.VMEM((tq, hw), f32), pltpu.VMEM((tq, hw), f32),
                        pltpu.VMEM((tq, LANES), f32), pltpu.VMEM((tq, LANES), f32),
                        pltpu.VMEM((tq, LANES), f32), pltpu.VMEM((tq, LANES), f32)],
        compiler_params=_params("parallel", "parallel", "parallel"),
        name="diff_attention",
    )(slopes, proj3, proj3, proj3, vec(lq1), vec(lk1), vec(lq2), vec(lk2), vec(g_subln))


def _conformer_kernel(a_ref, gate_ref, ah_ref, gateh_ref, wdw_ref, bdw_ref, gln_ref, bln_ref,
                      wpw_ref, o_ref, buf_ref, conv_ref, win_ref, *, ts, kw, halo, rt):
    i = pl.program_id(1)
    ch = a_ref.shape[-1]
    glu = a_ref[...].astype(f32) * jax.nn.sigmoid(gate_ref[...].astype(f32))
    glu_h = ah_ref[...].astype(f32) * jax.nn.sigmoid(gateh_ref[...].astype(f32))
    buf_ref[0:halo, :] = jnp.where(i > 0, glu_h, 0.0)
    buf_ref[halo:halo + ts, :] = glu
    off = halo - (kw - 1)
    def conv_chunk(c, carry):
        cs = pl.ds(pl.multiple_of(c * LANES, LANES), LANES)
        for r0 in range(0, ts, rt):
            acc = jnp.broadcast_to(bdw_ref[:, cs], (rt, LANES))
            for r in range(min(SUBLANES, kw)):
                span = (kw - 1 - r) // SUBLANES * SUBLANES
                win_ref[r, 0:rt + span, :] = buf_ref[r0 + off + r:r0 + off + r + rt + span, cs]
                for j in range(r, kw, SUBLANES):
                    acc = acc + wdw_ref[j:j + 1, cs] * win_ref[r, j - r:j - r + rt, :]
            conv_ref[r0:r0 + rt, cs] = acc
        return carry

    lax.fori_loop(0, ch // LANES, conv_chunk, 0)
    c = conv_ref[...]
    mu = jnp.mean(c, axis=-1, keepdims=True)
    xc = c - mu
    var = jnp.mean(xc * xc, axis=-1, keepdims=True)
    y = xc * lax.rsqrt(var + LN_EPS) * gln_ref[...] + bln_ref[...]
    act = y * jax.nn.sigmoid(y)
    o_ref[...] = jnp.dot(act.astype(bf16), wpw_ref[...],
                         preferred_element_type=f32).astype(o_ref.dtype)


def _conformer(proj3, col0, w_dw, b_dw, g_ln, b_ln, w_pw, cfg, tiles):
    b, s, _ = proj3.shape
    kw = cfg.conv_width
    ch = w_pw.shape[0]
    ts = min(tiles.conv_rows, s)
    halo = tiles.conv_halo
    assert halo >= kw - 1 and ts % halo == 0 and col0 % ch == 0
    cb = col0 // ch
    hb = ts // halo
    rt = math.gcd(ts, tiles.conv_acc_rows)
    win_rows = rt + (kw - 1) // SUBLANES * SUBLANES
    vec = lambda a: a.reshape(1, ch).astype(f32)
    small = lambda: pl.BlockSpec((1, ch), lambda bi, i: (0, 0))
    return pl.pallas_call(
        functools.partial(_conformer_kernel, ts=ts, kw=kw, halo=halo, rt=rt),
        grid=(b, s // ts),
        in_specs=[pl.BlockSpec((None, ts, ch), lambda bi, i: (bi, i, cb)),
                  pl.BlockSpec((None, ts, ch), lambda bi, i: (bi, i, cb + 1)),
                  pl.BlockSpec((None, halo, ch), lambda bi, i: (bi, jnp.maximum(i * hb - 1, 0), cb)),
                  pl.BlockSpec((None, halo, ch), lambda bi, i: (bi, jnp.maximum(i * hb - 1, 0), cb + 1)),
                  pl.BlockSpec((kw, ch), lambda bi, i: (0, 0)),
                  small(), small(), small(),
                  pl.BlockSpec((ch, ch), lambda bi, i: (0, 0))],
        out_specs=pl.BlockSpec((None, ts, ch), lambda bi, i: (bi, i, 0)),
        out_shape=jax.ShapeDtypeStruct((b, s, ch), bf16),
        scratch_shapes=[pltpu.VMEM((halo + ts, ch), f32), pltpu.VMEM((ts, ch), f32),
                        pltpu.VMEM((SUBLANES, win_rows, LANES), f32)],
        compiler_params=_params("parallel", "parallel"),
        name="conformer",
    )(proj3, proj3, proj3, proj3, w_dw.reshape(kw, ch).astype(f32), vec(b_dw), vec(g_ln), vec(b_ln),
      w_pw)


def _xattn_kernel(q_ref, k_ref, v_ref, o_ref, *, scale):
    s = lax.dot_general(q_ref[...], k_ref[...], (((1,), (1,)), ((), ())),
                        preferred_element_type=f32) * scale
    m = jnp.max(s, axis=-1, keepdims=True)
    p = jnp.exp(s - m)
    p = p / jnp.sum(p, axis=-1, keepdims=True)
    o_ref[...] = jnp.dot(p.astype(bf16), v_ref[...], preferred_element_type=f32).astype(o_ref.dtype)


def _cross_attention(q3, k3, v3, cfg, tiles):
    b, s, dm = q3.shape
    m = k3.shape[1]
    nh = cfg.x_heads
    hd = dm // nh
    tq = min(tiles.xattn_q, s)
    return pl.pallas_call(
        functools.partial(_xattn_kernel, scale=hd ** -0.5),
        grid=(b, s // tq, nh),
        in_specs=[pl.BlockSpec((None, tq, hd), lambda bi, i, h: (bi, i, h)),
                  pl.BlockSpec((None, m, hd), lambda bi, i, h: (bi, 0, h)),
                  pl.BlockSpec((None, m, hd), lambda bi, i, h: (bi, 0, h))],
        out_specs=pl.BlockSpec((None, tq, hd), lambda bi, i, h: (bi, i, h)),
        out_shape=jax.ShapeDtypeStruct((b, s, dm), bf16),
        compiler_params=_params("parallel", "parallel", "parallel"),
        name="cross_attention",
    )(q3, k3, v3)


def _pack_bf16_pair(lo, hi):
    def rounded(x):
        bits = lax.bitcast_convert_type(x, jnp.uint32)
        return bits + jnp.uint32(0x7FFF) + ((bits >> 16) & jnp.uint32(1))
    return (rounded(hi) & jnp.uint32(0xFFFF0000)) | (rounded(lo) >> 16)


def _unpack_bf16_pair(w):
    lo = lax.bitcast_convert_type(w << 16, f32)
    hi = lax.bitcast_convert_type(w & jnp.uint32(0xFFFF0000), f32)
    return lo, hi


def _store_token_chunks(o_ref, lo, hi, n_rows, n_sub, s0):
    pitch = n_sub + SLAB_PAD
    for s in range(lo.shape[1] // LANES):
        cs = slice(s * LANES, (s + 1) * LANES)
        o_ref[pl.ds(s0 + s, n_rows, stride=pitch), :] = _pack_bf16_pair(lo[:, cs], hi[:, cs])


def _zero_spare_rows(o_ref, n_rows, n_sub):
    o_ref[pl.ds(n_sub, n_rows, stride=n_sub + SLAB_PAD), :] = jnp.zeros((n_rows, LANES), o_ref.dtype)


def _router_kernel(x_ref, g_ref, wr_ref, br_ref, hg_ref, te_ref, tw_ref, cnt_ref, *, tm, n_sub, top_k):
    x = x_ref[...]
    ms = jnp.mean(x * x, axis=-1, keepdims=True)
    h = x * lax.rsqrt(ms + RMS_EPS) * g_ref[...]
    half = n_sub * LANES
    _store_token_chunks(hg_ref, h[:, :half], h[:, half:], tm, n_sub, 0)
    _zero_spare_rows(hg_ref, tm, n_sub)
    logits = jnp.dot(h.astype(bf16), wr_ref[...], preferred_element_type=f32) + br_ref[...]
    lane = lax.broadcasted_iota(jnp.int32, logits.shape, 1)
    lane_f = lane.astype(f32)
    vals, idxs = [], []
    for _ in range(top_k):
        m = jnp.max(logits, axis=-1, keepdims=True)
        idx = jnp.min(jnp.where(logits == m, lane_f, float(LANES)), axis=-1, keepdims=True)
        vals.append(m)
        idxs.append(idx)
        logits = jnp.where(lane_f == idx, NEG_INF * 2.0, logits)
    exps = [jnp.exp(v - vals[0]) for v in vals]
    denom = exps[0]
    for e in exps[1:]:
        denom = denom + e
    te = jnp.zeros(logits.shape, f32)
    tw = jnp.zeros(logits.shape, f32)
    hits = jnp.zeros(logits.shape, f32)
    for k in range(top_k):
        te = jnp.where(lane == k, idxs[k], te)
        tw = jnp.where(lane == k, exps[k] / denom, tw)
        hits = hits + jnp.where(lane_f == idxs[k], 1.0, 0.0)
    te_ref[...] = te.astype(jnp.int32)
    tw_ref[...] = tw
    tile_counts = jnp.sum(hits, axis=0, keepdims=True)
    row8 = lax.broadcasted_iota(jnp.int32, cnt_ref.shape, 0)
    cnt_ref[...] = jnp.where(row8 == 0, tile_counts, 0.0).astype(jnp.int32)


def _norm_router(x2d, g, w_router, b_router, cfg, tiles):
    n, d = x2d.shape
    e = w_router.shape[1]
    assert e <= LANES and d % (2 * LANES) == 0
    n_sub = d // (2 * LANES)
    pitch = n_sub + SLAB_PAD
    tm = min(tiles.router_rows, n)
    wr = jnp.zeros((d, LANES), bf16).at[:, :e].set(w_router.astype(bf16))
    br = jnp.full((1, LANES), NEG_INF, f32).at[0, :e].set(b_router.astype(f32))
    hg, te, tw, cnt = pl.pallas_call(
        functools.partial(_router_kernel, tm=tm, n_sub=n_sub, top_k=cfg.top_k),
        grid=(n // tm,),
        in_specs=[pl.BlockSpec((tm, d), lambda i: (i, 0)),
                  pl.BlockSpec((1, d), lambda i: (0, 0)),
                  pl.BlockSpec((d, LANES), lambda i: (0, 0)),
                  pl.BlockSpec((1, LANES), lambda i: (0, 0))],
        out_specs=[pl.BlockSpec((tm * pitch, LANES), lambda i: (i, 0)),
                   pl.BlockSpec((tm, LANES), lambda i: (i, 0)),
                   pl.BlockSpec((tm, LANES), lambda i: (i, 0)),
                   pl.BlockSpec((SUBLANES, LANES), lambda i: (i, 0))],
        out_shape=[jax.ShapeDtypeStruct((n * pitch, LANES), jnp.uint32),
                   jax.ShapeDtypeStruct((n, LANES), jnp.int32),
                   jax.ShapeDtypeStruct((n, LANES), f32),
                   jax.ShapeDtypeStruct((n // tm * SUBLANES, LANES), jnp.int32)],
        compiler_params=_params("parallel"),
        name="norm_router",
    )(x2d, g.reshape(1, d).astype(f32), wr, br)
    counts = jnp.sum(cnt, axis=0)[:e]
    return hg, te[:, :cfg.top_k], tw[:, :cfg.top_k], counts


def _routing_plan(top_e, gate_w, counts, blk):
    i32 = jnp.int32
    n_tok, top_k = top_e.shape
    n_experts = counts.shape[0]
    n_slot = n_tok * top_k
    slot_e = top_e.reshape(n_slot)
    slot_w = gate_w.reshape(n_slot)
    order = jnp.argsort(slot_e, stable=True).astype(i32)
    inv = jnp.argsort(order).astype(i32)
    experts = jnp.arange(n_experts, dtype=i32)
    padded = (counts + blk - 1) // blk * blk
    pad_end = jnp.cumsum(padded)
    pad_start = pad_end - padded
    start = jnp.cumsum(counts) - counts
    shift = pad_start - start
    n_blocks = -(-n_slot // blk) + n_experts
    n_rows = n_blocks * blk
    n_used = pad_end[-1] // blk
    last_e = jnp.max(jnp.where(counts > 0, experts, 0))
    block_start = jnp.arange(n_blocks, dtype=i32) * blk
    block_e = jnp.minimum(jnp.sum((pad_end[None, :] <= block_start[:, None]).astype(i32), axis=1), last_e)
    row = (block_start[:, None] + jnp.arange(blk, dtype=i32)[None, :])
    in_expert = row - pad_start[block_e][:, None]
    valid = (in_expert < counts[block_e][:, None]) & (row < pad_end[-1])
    row_slot = order[jnp.where(valid, row - shift[block_e][:, None], 0).reshape(n_rows)]
    valid = valid.reshape(n_rows)
    row_tok = jnp.where(valid, row_slot // top_k, 0)
    row_w = jnp.where(valid, slot_w[row_slot], 0.0)
    pos = inv + shift[slot_e]
    live = jnp.where(counts > 0, experts, n_experts)
    later = jnp.concatenate([lax.cummin(live[::-1])[::-1][1:], jnp.full((1,), n_experts, i32)])
    next_live = jnp.where(later < n_experts, later, -1)
    prev_e = jnp.concatenate([jnp.full((1,), -1, i32), block_e[:-1]])
    run_first = ((block_e != prev_e) & (block_start < pad_end[-1])).astype(i32)
    misc = jnp.stack([n_used, jnp.min(live)]).astype(i32)
    return row_tok, row_w.reshape(n_rows, 1), (block_e, run_first, next_live[block_e], misc), pos


def _gather_kernel(tok_ref, nu_ref, hg_hbm, o_ref, x3a_ref, x3b_ref, sem, *, blk, n_sub, unroll):
    b = pl.program_id(0)
    n_used = nu_ref[0]
    pitch = n_sub + SLAB_PAD
    bufs = (x3a_ref, x3b_ref)

    def issue(block_idx, slot):
        def body(r0, carry):
            for q in range(unroll):
                r = r0 * unroll + q
                tok = tok_ref[block_idx * blk + r]
                pltpu.make_async_copy(hg_hbm.at[pl.ds(tok * pitch, n_sub)], bufs[slot].at[:, r, :],
                                      sem.at[slot]).start()
            return carry
        lax.fori_loop(0, blk // unroll, body, 0)

    def wait_all(slot):
        pltpu.make_async_copy(bufs[slot], bufs[slot], sem.at[slot]).wait()

    @pl.when(b < n_used)
    def _():
        @pl.when(b == 0)
        def _():
            issue(0, 0)
        for slot in range(2):
            @pl.when(b % 2 == slot)
            def _():
                @pl.when(b + 1 < n_used)
                def _():
                    issue(b + 1, 1 - slot)
                wait_all(slot)
                for s in range(n_sub):
                    lo, hi = _unpack_bf16_pair(bufs[slot][s])
                    o_ref[:, s * LANES:(s + 1) * LANES] = lo.astype(o_ref.dtype)
                    o_ref[:, (n_sub + s) * LANES:(n_sub + s + 1) * LANES] = hi.astype(o_ref.dtype)

    @pl.when(b >= n_used)
    def _():
        o_ref[...] = jnp.zeros(o_ref.shape, o_ref.dtype)


def _stream_expert_weights(b, f, nf, be_ref, first_ref, nxt_ref, first_e, copies, stages, works):
    @pl.when(first_ref[b] == 1)
    def _():
        e = be_ref[b]

        @pl.when((f == 0) & (b == 0))
        def _():
            for copy in copies:
                copy(e, f).start()
        for copy in copies:
            copy(e, f).wait()
        for stage, work in zip(stages, works):
            work[...] = stage[...].astype(work.dtype)
        more = nxt_ref[b] >= 0

        @pl.when(more | (f + 1 < nf))
        def _():
            e2 = jnp.where(more, nxt_ref[b], first_e)
            f2 = jnp.where(more, f, f + 1)
            for copy in copies:
                copy(e2, f2).start()


def _gate_up_kernel(be_ref, first_ref, nxt_ref, misc_ref, x_ref, wg_hbm, bg_ref, wu_hbm, bu_ref, o_ref,
                    sg_ref, su_ref, wg_ref, wu_ref, sem, *, tf, nf):
    f = pl.program_id(0)
    b = pl.program_id(1)
    n_used, first_e = misc_ref[0], misc_ref[1]

    def tile_copy(hbm, stage, k):
        return lambda e, fi: pltpu.make_async_copy(
            hbm.at[e, :, pl.ds(pl.multiple_of(fi * tf, tf), tf)], stage, sem.at[k])

    @pl.when(b < n_used)
    def _():
        _stream_expert_weights(b, f, nf, be_ref, first_ref, nxt_ref, first_e,
                               (tile_copy(wg_hbm, sg_ref, 0), tile_copy(wu_hbm, su_ref, 1)),
                               (sg_ref, su_ref), (wg_ref, wu_ref))
        x = x_ref[...]
        g = jnp.minimum(jnp.dot(x, wg_ref[...], preferred_element_type=f32) + bg_ref[...], SWIGLU_LIMIT)
        up = jnp.clip(jnp.dot(x, wu_ref[...], preferred_element_type=f32) + bu_ref[...],
                      -SWIGLU_LIMIT, SWIGLU_LIMIT)
        o_ref[...] = ((up + 1.0) * (g * jax.nn.sigmoid(SWIGLU_ALPHA * g))).astype(o_ref.dtype)

    @pl.when(b >= n_used)
    def _():
        o_ref[...] = jnp.zeros(o_ref.shape, o_ref.dtype)


def _down_kernel(be_ref, first_ref, nxt_ref, misc_ref, a_ref, wd_hbm, bd_ref, rw_ref, o_ref,
                 sd_ref, wd_ref, sem, *, blk, n_sub, tc):
    b = pl.program_id(0)
    half = n_sub * LANES
    n_used, first_e = misc_ref[0], misc_ref[1]

    @pl.when(b < n_used)
    def _():
        copy = lambda e, fi: pltpu.make_async_copy(wd_hbm.at[e], sd_ref, sem.at[0])
        _stream_expert_weights(b, 0, 1, be_ref, first_ref, nxt_ref, first_e, (copy,), (sd_ref,), (wd_ref,))
        a = a_ref[...]
        rw = rw_ref[...]
        for c in range(half // tc):
            ys = []
            for c0 in (c * tc, half + c * tc):
                cs = slice(c0, c0 + tc)
                ys.append((jnp.dot(a, wd_ref[:, cs], preferred_element_type=f32) + bd_ref[:, cs]) * rw)
            _store_token_chunks(o_ref, ys[0], ys[1], blk, n_sub, c * tc // LANES)
        _zero_spare_rows(o_ref, blk, n_sub)

    @pl.when(b >= n_used)
    def _():
        o_ref[...] = jnp.zeros(o_ref.shape, o_ref.dtype)


def _experts(hg, row_tok, row_w, runs, wg, bg, wu, bu, wd, bd, tiles):
    block_e, run_first, run_next, misc = runs
    n_e, d, ff = wg.shape
    n_sub = d // (2 * LANES)
    pitch = n_sub + SLAB_PAD
    blk = tiles.expert_rows
    n_rows = row_tok.shape[0]
    n_blocks = n_rows // blk
    tf = min(tiles.expert_cols, ff)
    nf = ff // tf
    xs = pl.pallas_call(
        functools.partial(_gather_kernel, blk=blk, n_sub=n_sub, unroll=math.gcd(blk, tiles.gather_unroll)),
        grid_spec=pltpu.PrefetchScalarGridSpec(
            num_scalar_prefetch=2,
            grid=(n_blocks,),
            in_specs=[pl.BlockSpec(memory_space=pl.ANY)],
            out_specs=pl.BlockSpec((blk, d), lambda b, tok, nu: (b, 0)),
            scratch_shapes=[pltpu.VMEM((n_sub, blk, LANES), jnp.uint32),
                            pltpu.VMEM((n_sub, blk, LANES), jnp.uint32),
                            pltpu.SemaphoreType.DMA((2,))]),
        out_shape=jax.ShapeDtypeStruct((n_rows, d), bf16),
        compiler_params=_params("arbitrary"),
        name="expert_gather",
    )(row_tok, misc, hg)
    xb = lambda f, b, be, fi, nx, ms: (jnp.minimum(b, ms[0] - 1), 0)
    bias = lambda f, b, be, fi, nx, ms: (be[b], 0, f)
    act = pl.pallas_call(
        functools.partial(_gate_up_kernel, tf=tf, nf=nf),
        grid_spec=pltpu.PrefetchScalarGridSpec(
            num_scalar_prefetch=4,
            grid=(nf, n_blocks),
            in_specs=[pl.BlockSpec((blk, d), xb),
                      pl.BlockSpec(memory_space=pl.ANY),
                      pl.BlockSpec((None, 1, tf), bias),
                      pl.BlockSpec(memory_space=pl.ANY),
                      pl.BlockSpec((None, 1, tf), bias)],
            out_specs=pl.BlockSpec((blk, tf), lambda f, b, be, fi, nx, ms: (b, f)),
            scratch_shapes=[pltpu.VMEM((d, tf), f32), pltpu.VMEM((d, tf), f32),
                            pltpu.VMEM((d, tf), bf16), pltpu.VMEM((d, tf), bf16),
                            pltpu.SemaphoreType.DMA((2,))]),
        out_shape=jax.ShapeDtypeStruct((n_rows, ff), bf16),
        compiler_params=_params("arbitrary", "arbitrary"),
        name="expert_gate_up",
    )(block_e, run_first, run_next, misc, xs, wg, bg.reshape(n_e, 1, ff).astype(f32), wu,
      bu.reshape(n_e, 1, ff).astype(f32))
    tc = min(tiles.down_cols, d)
    return pl.pallas_call(
        functools.partial(_down_kernel, blk=blk, n_sub=n_sub, tc=tc),
        grid_spec=pltpu.PrefetchScalarGridSpec(
            num_scalar_prefetch=4,
            grid=(n_blocks,),
            in_specs=[pl.BlockSpec((blk, ff), lambda b, be, fi, nx, ms: (jnp.minimum(b, ms[0] - 1), 0)),
                      pl.BlockSpec(memory_space=pl.ANY),
                      pl.BlockSpec((None, 1, d), lambda b, be, fi, nx, ms: (be[b], 0, 0)),
                      pl.BlockSpec((blk, 1), lambda b, be, fi, nx, ms: (b, 0))],
            out_specs=pl.BlockSpec((blk * pitch, LANES), lambda b, be, fi, nx, ms: (b, 0)),
            scratch_shapes=[pltpu.VMEM((ff, d), f32), pltpu.VMEM((ff, d), bf16),
                            pltpu.SemaphoreType.DMA((1,))]),
        out_shape=jax.ShapeDtypeStruct((n_rows * pitch, LANES), jnp.uint32),
        compiler_params=_params("arbitrary"),
        name="expert_down",
    )(block_e, run_first, run_next, misc, act, wd, bd.reshape(n_e, 1, d).astype(f32), row_w)


def _combine_kernel(pos_ref, x_ref, ys_hbm, g_ref, o_ref, b0_ref, b1_ref, sem, *, tm, n_sub, top_k,
                    final_norm):
    i = pl.program_id(0)
    nb = pl.num_programs(0)
    bufs = (b0_ref, b1_ref)
    pitch = n_sub + SLAB_PAD

    def issue(block_idx, slot):
        def body(t0, carry):
            for q in range(2):
                t = t0 * 2 + q
                for k in range(top_k):
                    row = pos_ref[(block_idx * tm + t) * top_k + k]
                    pltpu.make_async_copy(ys_hbm.at[pl.ds(row * pitch, n_sub)],
                                          bufs[slot].at[:, k * tm + t, :], sem.at[slot]).start()
            return carry
        lax.fori_loop(0, tm // 2, body, 0)

    def wait_all(slot):
        pltpu.make_async_copy(bufs[slot], bufs[slot], sem.at[slot]).wait()

    @pl.when(i == 0)
    def _():
        issue(0, 0)

    for slot in range(2):
        @pl.when(i % 2 == slot)
        def _():
            @pl.when(i + 1 < nb)
            def _():
                issue(i + 1, 1 - slot)
            wait_all(slot)
            for s in range(n_sub):
                cs_lo = slice(s * LANES, (s + 1) * LANES)
                cs_hi = slice((n_sub + s) * LANES, (n_sub + s + 1) * LANES)
                acc_lo = x_ref[:, cs_lo]
                acc_hi = x_ref[:, cs_hi]
                for k in range(top_k):
                    lo, hi = _unpack_bf16_pair(bufs[slot][s, k * tm:(k + 1) * tm, :])
                    acc_lo = acc_lo + lo
                    acc_hi = acc_hi + hi
                o_ref[:, cs_lo] = acc_lo
                o_ref[:, cs_hi] = acc_hi

    if final_norm:
        x = o_ref[...]
        ms = jnp.mean(x * x, axis=-1, keepdims=True)
        o_ref[...] = x * lax.rsqrt(ms + RMS_EPS) * g_ref[...]


def _combine(x2d, ys, pos, g_final, final_norm, cfg, tiles):
    n, d = x2d.shape
    n_sub = d // (2 * LANES)
    tm = min(tiles.combine_rows, n)
    top_k = cfg.top_k
    grid_spec = pltpu.PrefetchScalarGridSpec(
        num_scalar_prefetch=1,
        grid=(n // tm,),
        in_specs=[pl.BlockSpec((tm, d), lambda i, pos: (i, 0)),
                  pl.BlockSpec(memory_space=pl.ANY),
                  pl.BlockSpec((1, d), lambda i, pos: (0, 0))],
        out_specs=pl.BlockSpec((tm, d), lambda i, pos: (i, 0)),
        scratch_shapes=[pltpu.VMEM((n_sub, top_k * tm, LANES), jnp.uint32),
                        pltpu.VMEM((n_sub, top_k * tm, LANES), jnp.uint32),
                        pltpu.SemaphoreType.DMA((2,))],
    )
    return pl.pallas_call(
        functools.partial(_combine_kernel, tm=tm, n_sub=n_sub, top_k=top_k, final_norm=final_norm),
        grid_spec=grid_spec,
        out_shape=jax.ShapeDtypeStruct((n, d), f32),
        compiler_params=_params("arbitrary"),
        name="moe_combine",
    )(pos, x2d, ys, g_final.reshape(1, d).astype(f32))


def _layer(x, mem, p, l, is_last, g_final, cfg, tiles):
    b, s, d = x.shape
    n = b * s
    m_len = mem.shape[1]
    diff_width = d // 2
    conv_ch = d - diff_width
    x2d = x.reshape(n, d)

    h = _rmsnorm(x2d, p['g_mix'][l], tiles.norm_rows)
    head_dim = p['lambda_q1'].shape[-1]
    in_cols = p['w_in'].shape[-1]
    q_scale = jnp.where(jnp.arange(in_cols) < diff_width, head_dim ** -0.5 * LOG2E, 1.0)
    proj = _matmul([h], p['w_in'][l], None, bf16, tiles.mm_rows, tiles.mm_cols, "in_proj",
                   col_scale=q_scale)
    proj3 = proj.reshape(b, s, -1)
    lam_init = 0.8 - 0.6 * math.exp(-0.3 * l)
    a_out = _diff_attention(proj3, p['lambda_q1'][l], p['lambda_k1'][l], p['lambda_q2'][l],
                            p['lambda_k2'][l], p['g_subln'][l], lam_init, cfg, tiles)
    c_out = _conformer(proj3, 3 * diff_width, p['w_dw'][l], p['b_dw'][l], p['g_conv_ln'][l],
                       p['b_conv_ln'][l], p['w_conv_pw'][l].astype(bf16), cfg, tiles)
    assert diff_width == conv_ch
    x2d = _matmul([a_out.reshape(n, diff_width), c_out.reshape(n, conv_ch)],
                  p['w_out'][l], x2d, f32, tiles.mm_rows, tiles.mm_cols, "out_proj")

    hc = _rmsnorm(x2d, p['g_cross'][l], tiles.norm_rows)
    mem_n = _rmsnorm(mem.reshape(b * m_len, d), p['g_mem'][l], tiles.norm_rows)
    qc = _matmul([hc], p['w_cq'][l], None, bf16, tiles.mm_rows, tiles.mm_cols, "xq_proj")
    kc = _matmul([mem_n], p['w_ck'][l], None, bf16, tiles.mm_rows, tiles.mm_cols, "xk_proj")
    vc = _matmul([mem_n], p['w_cv'][l], None, bf16, tiles.mm_rows, tiles.mm_cols, "xv_proj")
    oc = _cross_attention(qc.reshape(b, s, d), kc.reshape(b, m_len, d), vc.reshape(b, m_len, d),
                          cfg, tiles)
    x2d = _matmul([oc.reshape(n, d)], p['w_co'][l], x2d, f32, tiles.mm_rows, tiles.mm_cols, "xo_proj")

    hg, top_e, gate_w, counts = _norm_router(x2d, p['g_ffn'][l], p['w_router'][l], p['b_router'][l],
                                             cfg, tiles)
    row_tok, row_w, runs, pos = _routing_plan(top_e, gate_w, counts, tiles.expert_rows)
    ys = _experts(hg, row_tok, row_w, runs, p['w_gate'][l], p['b_gate'][l], p['w_up'][l], p['b_up'][l],
                  p['w_down'][l], p['b_down'][l], tiles)
    x2d = _combine(x2d, ys, pos, g_final, is_last, cfg, tiles)
    return x2d.reshape(b, s, d)


def _forward(x, mem, params, g_final, cfg, tiles):
    depth = params['g_mix'].shape[0]
    for l in range(depth):
        x = _layer(x, mem, params, l, l == depth - 1, g_final, cfg, tiles)
    return x


def kernel(x, mem, g_mix, w_in, lambda_q1, lambda_k1, lambda_q2, lambda_k2, g_subln, w_dw, b_dw,
           g_conv_ln, b_conv_ln, w_conv_pw, w_out, g_cross, g_mem, w_cq, w_ck, w_cv, w_co, g_ffn,
           w_router, b_router, w_gate, b_gate, w_up, b_up, w_down, b_down, g_final):
    params = dict(g_mix=g_mix, w_in=w_in, lambda_q1=lambda_q1, lambda_k1=lambda_k1,
                  lambda_q2=lambda_q2, lambda_k2=lambda_k2, g_subln=g_subln, w_dw=w_dw, b_dw=b_dw,
                  g_conv_ln=g_conv_ln, b_conv_ln=b_conv_ln, w_conv_pw=w_conv_pw, w_out=w_out,
                  g_cross=g_cross, g_mem=g_mem, w_cq=w_cq, w_ck=w_ck, w_cv=w_cv, w_co=w_co,
                  g_ffn=g_ffn, w_router=w_router, b_router=b_router, w_gate=w_gate, b_gate=b_gate,
                  w_up=w_up, b_up=b_up, w_down=w_down, b_down=b_down)
    return _forward(x, mem, params, g_final, Cfg(), Tiles())
```

```python
import functools
import math
from typing import NamedTuple

import jax
import jax.numpy as jnp
from jax import lax
from jax.experimental import pallas as pl
from jax.experimental.pallas import tpu as pltpu

f32 = jnp.float32
bf16 = jnp.bfloat16

RMS_EPS = 1e-6
LN_EPS = 1e-5
NEG_INF = -1e30
SWIGLU_LIMIT = 7.0
SWIGLU_ALPHA = 1.702
LOG2E = 1.4426950408889634

LANES = 128
SUBLANES = 8
SLAB_PAD = 1
VMEM_LIMIT_BYTES = 56 * 1024 * 1024


class Cfg(NamedTuple):
    diff_heads: int = 8
    conv_width: int = 31
    x_heads: int = 4
    top_k: int = 4


class Tiles(NamedTuple):
    norm_rows: int = 256
    mm_rows: int = 1024
    mm_cols: int = 512
    attn_q: int = 1024
    conv_rows: int = 256
    conv_halo: int = 32
    conv_acc_rows: int = 128
    xattn_q: int = 512
    router_rows: int = 256
    expert_rows: int = 256
    gather_unroll: int = 8
    expert_cols: int = 768
    down_cols: int = 512
    combine_rows: int = 128


def _params(*sem):
    return pltpu.CompilerParams(dimension_semantics=sem, vmem_limit_bytes=VMEM_LIMIT_BYTES)


def _rmsnorm_kernel(x_ref, g_ref, o_ref):
    x = x_ref[...]
    ms = jnp.mean(x * x, axis=-1, keepdims=True)
    o_ref[...] = (x * lax.rsqrt(ms + RMS_EPS) * g_ref[...]).astype(o_ref.dtype)


def _rmsnorm(x2d, g, rows):
    n, d = x2d.shape
    rows = min(rows, n)
    return pl.pallas_call(
        _rmsnorm_kernel,
        grid=(n // rows,),
        in_specs=[pl.BlockSpec((rows, d), lambda i: (i, 0)),
                  pl.BlockSpec((1, d), lambda i: (0, 0))],
        out_specs=pl.BlockSpec((rows, d), lambda i: (i, 0)),
        out_shape=jax.ShapeDtypeStruct((n, d), bf16),
        compiler_params=_params("parallel"),
        name="rmsnorm",
    )(x2d, g.reshape(1, d).astype(f32))


def _matmul_kernel(*refs, n_lhs, has_res, has_scale):
    a_refs = refs[:n_lhs]
    w_refs = refs[n_lhs:2 * n_lhs]
    extra = list(refs[2 * n_lhs:-1 - n_lhs])
    o_ref = refs[-1 - n_lhs]
    wb_refs = refs[len(refs) - n_lhs:]

    @pl.when(pl.program_id(1) == 0)
    def _():
        for w_ref, wb_ref in zip(w_refs, wb_refs):
            wb_ref[...] = w_ref[...].astype(bf16)

    acc = jnp.dot(a_refs[0][...], wb_refs[0][...], preferred_element_type=f32)
    for p in range(1, n_lhs):
        acc = acc + jnp.dot(a_refs[p][...], wb_refs[p][...], preferred_element_type=f32)
    if has_res:
        acc = acc + extra.pop(0)[...]
    if has_scale:
        acc = acc * extra.pop(0)[...]
    o_ref[...] = acc.astype(o_ref.dtype)


def _matmul(lhs_list, w, res, out_dtype, rows, cols, name, col_scale=None):
    n_lhs = len(lhs_list)
    m, kp = lhs_list[0].shape
    nc = w.shape[1]
    assert w.shape[0] == n_lhs * kp
    rows = min(rows, m)
    cols = min(cols, nc)
    in_specs = [pl.BlockSpec((rows, kp), lambda j, i: (i, 0)) for _ in range(n_lhs)]
    in_specs += [pl.BlockSpec((kp, cols), lambda j, i, p=p: (p, j)) for p in range(n_lhs)]
    args = list(lhs_list) + [w] * n_lhs
    if res is not None:
        in_specs.append(pl.BlockSpec((rows, cols), lambda j, i: (i, j)))
        args.append(res)
    if col_scale is not None:
        in_specs.append(pl.BlockSpec((1, cols), lambda j, i: (0, j)))
        args.append(col_scale.reshape(1, nc).astype(f32))
    return pl.pallas_call(
        functools.partial(_matmul_kernel, n_lhs=n_lhs, has_res=res is not None,
                          has_scale=col_scale is not None),
        grid=(nc // cols, m // rows),
        in_specs=in_specs,
        out_specs=pl.BlockSpec((rows, cols), lambda j, i: (i, j)),
        out_shape=jax.ShapeDtypeStruct((m, nc), out_dtype),
        scratch_shapes=[pltpu.VMEM((kp, cols), bf16) for _ in range(n_lhs)],
        compiler_params=_params("parallel", "arbitrary"),
        name=name,
    )(*args)


def _bf16_part(x):
    bits = lax.bitcast_convert_type(x, jnp.uint32) & jnp.uint32(0xFFFF0000)
    return lax.bitcast_convert_type(bits, f32)


def _lane_tile(x, n):
    return x if n == 1 else jnp.concatenate([x] * n, axis=1)


def _diff_attn_kernel(slopes_ref, q_ref, k_ref, v_ref, lq1_ref, lk1_ref, lq2_ref, lk2_ref, g_ref,
                      o_ref, qa1_ref, qa2_ref, ka1_ref, ka2_ref, acc1_ref, acc2_ref,
                      m1_ref, l1_ref, m2_ref, l2_ref, *, tq, d, lam_init):
    h = pl.program_id(1)
    qi = pl.program_id(2)
    tk = tq
    slope2 = slopes_ref[h]
    lane_q = lax.broadcasted_iota(jnp.int32, (tq, d), 1)
    ones_cols = jnp.where(lane_q < 3, 1.0, 0.0).astype(bf16)
    q = q_ref[...]
    qa1_ref[:, :d] = q[:, :d]
    qa1_ref[:, d:] = ones_cols
    qa2_ref[:, :d] = q[:, d:]
    qa2_ref[:, d:] = ones_cols
    lane_k = lax.broadcasted_iota(jnp.int32, (tk, d), 1)
    v0 = slope2 * lax.broadcasted_iota(jnp.int32, (tk, d), 0).astype(f32)
    hi = _bf16_part(v0)
    r1 = v0 - hi
    lo = _bf16_part(r1)
    lo2 = r1 - lo
    bias_cols = jnp.where(lane_k == 0, hi, jnp.where(lane_k == 1, lo, jnp.where(lane_k == 2, lo2, 0.0)))
    ka1_ref[:, d:] = bias_cols.astype(bf16)
    ka2_ref[:, d:] = bias_cols.astype(bf16)
    rel = (lax.broadcasted_iota(jnp.int32, (tq, tk), 0)
           - lax.broadcasted_iota(jnp.int32, (tq, tk), 1))
    maps = ((qa1_ref, ka1_ref, m1_ref, l1_ref, acc1_ref), (qa2_ref, ka2_ref, m2_ref, l2_ref, acc2_ref))
    for _, _, m_ref, l_ref, acc_ref in maps:
        m_ref[...] = jnp.full(m_ref.shape, NEG_INF, f32)
        l_ref[...] = jnp.zeros(l_ref.shape, f32)
        acc_ref[...] = jnp.zeros(acc_ref.shape, f32)

    def block(j, masked):
        k0 = pl.multiple_of(j * tk, tk)
        kb = k_ref[pl.ds(k0, tk), :]
        vb = v_ref[pl.ds(k0, tk), :]
        cj = slope2 * ((j - qi) * tk).astype(f32)
        for c, (qa_ref, ka_ref, m_ref, l_ref, acc_ref) in enumerate(maps):
            ka_ref[:, :d] = kb[:, c * d:(c + 1) * d]
            s = lax.dot_general(qa_ref[...], ka_ref[...], (((1,), (1,)), ((), ())),
                                preferred_element_type=f32)
            if masked:
                s = jnp.where(rel >= 0, s, NEG_INF)
            m_old = m_ref[...]
            m_new = jnp.maximum(m_old, jnp.max(s, axis=-1, keepdims=True) + cj)
            p = jnp.exp2(s - _lane_tile(m_new - cj, tk // LANES))
            alpha = jnp.exp2(m_old - m_new)
            l_ref[...] = alpha * l_ref[...] + jnp.sum(p, axis=-1, keepdims=True)
            acc_ref[...] = (_lane_tile(alpha, acc_ref.shape[1] // LANES) * acc_ref[...]
                            + jnp.dot(p.astype(bf16), vb, preferred_element_type=f32))
            m_ref[...] = m_new

    def off_diag(j, carry):
        block(j, False)
        return carry

    lax.fori_loop(0, qi, off_diag, 0)
    block(qi, True)

    lam = (jnp.exp(jnp.sum(lq1_ref[...] * lk1_ref[...], axis=-1, keepdims=True))
           - jnp.exp(jnp.sum(lq2_ref[...] * lk2_ref[...], axis=-1, keepdims=True)) + lam_init)
    reps = acc1_ref.shape[1] // LANES
    o = (acc1_ref[...] / _lane_tile(l1_ref[...], reps)
         - lam * (acc2_ref[...] / _lane_tile(l2_ref[...], reps)))
    ms = jnp.mean(o * o, axis=-1, keepdims=True)
    y = o * lax.rsqrt(ms + RMS_EPS) * g_ref[...] * (1.0 - lam_init)
    o_ref[...] = y.astype(o_ref.dtype)


def _diff_attention(proj3, lq1, lk1, lq2, lk2, g_subln, lam_init, cfg, tiles):
    b, s, _ = proj3.shape
    nh = cfg.diff_heads
    d = lq1.shape[-1]
    hw = 2 * d
    tq = min(tiles.attn_q, s)
    slopes = jnp.exp2(-8.0 * jnp.arange(1, nh + 1, dtype=f32) / nh) * LOG2E
    vec = lambda a: a.reshape(1, -1).astype(f32)
    small = lambda n: pl.BlockSpec((1, n), lambda bi, hi, qi: (0, 0))
    return pl.pallas_call(
        functools.partial(_diff_attn_kernel, tq=tq, d=d, lam_init=lam_init),
        grid=(b, nh, s // tq),
        in_specs=[pl.BlockSpec(memory_space=pltpu.SMEM),
                  pl.BlockSpec((None, tq, hw), lambda bi, hi, qi: (bi, qi, hi)),
                  pl.BlockSpec((None, s, hw), lambda bi, hi, qi: (bi, 0, nh + hi)),
                  pl.BlockSpec((None, s, hw), lambda bi, hi, qi: (bi, 0, 2 * nh + hi)),
                  small(d), small(d), small(d), small(d), small(hw)],
        out_specs=pl.BlockSpec((None, tq, hw), lambda bi, hi, qi: (bi, qi, hi)),
        out_shape=jax.ShapeDtypeStruct((b, s, nh * hw), bf16),
        scratch_shapes=[pltpu.VMEM((tq, hw), bf16), pltpu.VMEM((tq, hw), bf16),
                        pltpu.VMEM((tq, hw), bf16), pltpu.VMEM((tq, hw), bf16),
                        pltpu.VMEM((tq, hw), f32), pltpu.VMEM((tq, hw), f32),
                        pltpu.VMEM((tq, LANES), f32), pltpu.VMEM((tq, LANES), f32),
                        pltpu.VMEM((tq, LANES), f32), pltpu.VMEM((tq, LANES), f32)],
        compiler_params=_params("parallel", "parallel", "parallel"),
        name="diff_attention",
    )(slopes, proj3, proj3, proj3, vec(lq1), vec(lk1), vec(lq2), vec(lk2), vec(g_subln))


def _conformer_kernel(a_ref, gate_ref, ah_ref, gateh_ref, wdw_ref, bdw_ref, gln_ref, bln_ref,
                      wpw_ref, o_ref, buf_ref, conv_ref, win_ref, *, ts, kw, halo, rt):
    i = pl.program_id(1)
    ch = a_ref.shape[-1]
    glu = a_ref[...].astype(f32) * jax.nn.sigmoid(gate_ref[...].astype(f32))
    glu_h = ah_ref[...].astype(f32) * jax.nn.sigmoid(gateh_ref[...].astype(f32))
    buf_ref[0:halo, :] = jnp.where(i > 0, glu_h, 0.0)
    buf_ref[halo:halo + ts, :] = glu
    off = halo - (kw - 1)
    def conv_chunk(c, carry):
        cs = pl.ds(pl.multiple_of(c * LANES, LANES), LANES)
        for r0 in range(0, ts, rt):
            acc = jnp.broadcast_to(bdw_ref[:, cs], (rt, LANES))
            for r in range(min(SUBLANES, kw)):
                span = (kw - 1 - r) // SUBLANES * SUBLANES
                win_ref[r, 0:rt + span, :] = buf_ref[r0 + off + r:r0 + off + r + rt + span, cs]
                for j in range(r, kw, SUBLANES):
                    acc = acc + wdw_ref[j:j + 1, cs] * win_ref[r, j - r:j - r + rt, :]
            conv_ref[r0:r0 + rt, cs] = acc
        return carry

    lax.fori_loop(0, ch // LANES, conv_chunk, 0)
    c = conv_ref[...]
    mu = jnp.mean(c, axis=-1, keepdims=True)
    xc = c - mu
    var = jnp.mean(xc * xc, axis=-1, keepdims=True)
    y = xc * lax.rsqrt(var + LN_EPS) * gln_ref[...] + bln_ref[...]
    act = y * jax.nn.sigmoid(y)
    o_ref[...] = jnp.dot(act.astype(bf16), wpw_ref[...],
                         preferred_element_type=f32).astype(o_ref.dtype)


def _conformer(proj3, col0, w_dw, b_dw, g_ln, b_ln, w_pw, cfg, tiles):
    b, s, _ = proj3.shape
    kw = cfg.conv_width
    ch = w_pw.shape[0]
    ts = min(tiles.conv_rows, s)
    halo = tiles.conv_halo
    assert halo >= kw - 1 and ts % halo == 0 and col0 % ch == 0
    cb = col0 // ch
    hb = ts // halo
    rt = math.gcd(ts, tiles.conv_acc_rows)
    win_rows = rt + (kw - 1) // SUBLANES * SUBLANES
    vec = lambda a: a.reshape(1, ch).astype(f32)
    small = lambda: pl.BlockSpec((1, ch), lambda bi, i: (0, 0))
    return pl.pallas_call(
        functools.partial(_conformer_kernel, ts=ts, kw=kw, halo=halo, rt=rt),
        grid=(b, s // ts),
        in_specs=[pl.BlockSpec((None, ts, ch), lambda bi, i: (bi, i, cb)),
                  pl.BlockSpec((None, ts, ch), lambda bi, i: (bi, i, cb + 1)),
                  pl.BlockSpec((None, halo, ch), lambda bi, i: (bi, jnp.maximum(i * hb - 1, 0), cb)),
                  pl.BlockSpec((None, halo, ch), lambda bi, i: (bi, jnp.maximum(i * hb - 1, 0), cb + 1)),
                  pl.BlockSpec((kw, ch), lambda bi, i: (0, 0)),
                  small(), small(), small(),
                  pl.BlockSpec((ch, ch), lambda bi, i: (0, 0))],
        out_specs=pl.BlockSpec((None, ts, ch), lambda bi, i: (bi, i, 0)),
        out_shape=jax.ShapeDtypeStruct((b, s, ch), bf16),
        scratch_shapes=[pltpu.VMEM((halo + ts, ch), f32), pltpu.VMEM((ts, ch), f32),
                        pltpu.VMEM((SUBLANES, win_rows, LANES), f32)],
        compiler_params=_params("parallel", "parallel"),
        name="conformer",
    )(proj3, proj3, proj3, proj3, w_dw.reshape(kw, ch).astype(f32), vec(b_dw), vec(g_ln), vec(b_ln),
      w_pw)


def _xattn_kernel(q_ref, k_ref, v_ref, o_ref, *, scale):
    s = lax.dot_general(q_ref[...], k_ref[...], (((1,), (1,)), ((), ())),
                        preferred_element_type=f32) * scale
    m = jnp.max(s, axis=-1, keepdims=True)
    p = jnp.exp(s - m)
    p = p / jnp.sum(p, axis=-1, keepdims=True)
    o_ref[...] = jnp.dot(p.astype(bf16), v_ref[...], preferred_element_type=f32).astype(o_ref.dtype)


def _cross_attention(q3, k3, v3, cfg, tiles):
    b, s, dm = q3.shape
    m = k3.shape[1]
    nh = cfg.x_heads
    hd = dm // nh
    tq = min(tiles.xattn_q, s)
    return pl.pallas_call(
        functools.partial(_xattn_kernel, scale=hd ** -0.5),
        grid=(b, s // tq, nh),
        in_specs=[pl.BlockSpec((None, tq, hd), lambda bi, i, h: (bi, i, h)),
                  pl.BlockSpec((None, m, hd), lambda bi, i, h: (bi, 0, h)),
                  pl.BlockSpec((None, m, hd), lambda bi, i, h: (bi, 0, h))],
        out_specs=pl.BlockSpec((None, tq, hd), lambda bi, i, h: (bi, i, h)),
        out_shape=jax.ShapeDtypeStruct((b, s, dm), bf16),
        compiler_params=_params("parallel", "parallel", "parallel"),
        name="cross_attention",
    )(q3, k3, v3)


def _pack_bf16_pair(lo, hi):
    def rounded(x):
        bits = lax.bitcast_convert_type(x, jnp.uint32)
        return bits + jnp.uint32(0x7FFF) + ((bits >> 16) & jnp.uint32(1))
    return (rounded(hi) & jnp.uint32(0xFFFF0000)) | (rounded(lo) >> 16)


def _unpack_bf16_pair(w):
    lo = lax.bitcast_convert_type(w << 16, f32)
    hi = lax.bitcast_convert_type(w & jnp.uint32(0xFFFF0000), f32)
    return lo, hi


def _store_token_chunks(o_ref, lo, hi, n_rows, n_sub, s0):
    pitch = n_sub + SLAB_PAD
    for s in range(lo.shape[1] // LANES):
        cs = slice(s * LANES, (s + 1) * LANES)
        o_ref[pl.ds(s0 + s, n_rows, stride=pitch), :] = _pack_bf16_pair(lo[:, cs], hi[:, cs])


def _zero_spare_rows(o_ref, n_rows, n_sub):
    o_ref[pl.ds(n_sub, n_rows, stride=n_sub + SLAB_PAD), :] = jnp.zeros((n_rows, LANES), o_ref.dtype)


def _router_kernel(x_ref, g_ref, wr_ref, br_ref, hg_ref, te_ref, tw_ref, cnt_ref, *, tm, n_sub, top_k):
    x = x_ref[...]
    ms = jnp.mean(x * x, axis=-1, keepdims=True)
    h = x * lax.rsqrt(ms + RMS_EPS) * g_ref[...]
    half = n_sub * LANES
    _store_token_chunks(hg_ref, h[:, :half], h[:, half:], tm, n_sub, 0)
    _zero_spare_rows(hg_ref, tm, n_sub)
    logits = jnp.dot(h.astype(bf16), wr_ref[...], preferred_element_type=f32) + br_ref[...]
    lane = lax.broadcasted_iota(jnp.int32, logits.shape, 1)
    lane_f = lane.astype(f32)
    vals, idxs = [], []
    for _ in range(top_k):
        m = jnp.max(logits, axis=-1, keepdims=True)
        idx = jnp.min(jnp.where(logits == m, lane_f, float(LANES)), axis=-1, keepdims=True)
        vals.append(m)
        idxs.append(idx)
        logits = jnp.where(lane_f == idx, NEG_INF * 2.0, logits)
    exps = [jnp.exp(v - vals[0]) for v in vals]
    denom = exps[0]
    for e in exps[1:]:
        denom = denom + e
    te = jnp.zeros(logits.shape, f32)
    tw = jnp.zeros(logits.shape, f32)
    hits = jnp.zeros(logits.shape, f32)
    for k in range(top_k):
        te = jnp.where(lane == k, idxs[k], te)
        tw = jnp.where(lane == k, exps[k] / denom, tw)
        hits = hits + jnp.where(lane_f == idxs[k], 1.0, 0.0)
    te_ref[...] = te.astype(jnp.int32)
    tw_ref[...] = tw
    tile_counts = jnp.sum(hits, axis=0, keepdims=True)
    row8 = lax.broadcasted_iota(jnp.int32, cnt_ref.shape, 0)
    cnt_ref[...] = jnp.where(row8 == 0, tile_counts, 0.0).astype(jnp.int32)


def _norm_router(x2d, g, w_router, b_router, cfg, tiles):
    n, d = x2d.shape
    e = w_router.shape[1]
    assert e <= LANES and d % (2 * LANES) == 0
    n_sub = d // (2 * LANES)
    pitch = n_sub + SLAB_PAD
    tm = min(tiles.router_rows, n)
    wr = jnp.zeros((d, LANES), bf16).at[:, :e].set(w_router.astype(bf16))
    br = jnp.full((1, LANES), NEG_INF, f32).at[0, :e].set(b_router.astype(f32))
    hg, te, tw, cnt = pl.pallas_call(
        functools.partial(_router_kernel, tm=tm, n_sub=n_sub, top_k=cfg.top_k),
        grid=(n // tm,),
        in_specs=[pl.BlockSpec((tm, d), lambda i: (i, 0)),
                  pl.BlockSpec((1, d), lambda i: (0, 0)),
                  pl.BlockSpec((d, LANES), lambda i: (0, 0)),
                  pl.BlockSpec((1, LANES), lambda i: (0, 0))],
        out_specs=[pl.BlockSpec((tm * pitch, LANES), lambda i: (i, 0)),
                   pl.BlockSpec((tm, LANES), lambda i: (i, 0)),
                   pl.BlockSpec((tm, LANES), lambda i: (i, 0)),
                   pl.BlockSpec((SUBLANES, LANES), lambda i: (i, 0))],
        out_shape=[jax.ShapeDtypeStruct((n * pitch, LANES), jnp.uint32),
                   jax.ShapeDtypeStruct((n, LANES), jnp.int32),
                   jax.ShapeDtypeStruct((n, LANES), f32),
                   jax.ShapeDtypeStruct((n // tm * SUBLANES, LANES), jnp.int32)],
        compiler_params=_params("parallel"),
        name="norm_router",
    )(x2d, g.reshape(1, d).astype(f32), wr, br)
    counts = jnp.sum(cnt, axis=0)[:e]
    return hg, te[:, :cfg.top_k], tw[:, :cfg.top_k], counts


def _routing_plan(top_e, gate_w, counts, blk):
    i32 = jnp.int32
    n_tok, top_k = top_e.shape
    n_experts = counts.shape[0]
    n_slot = n_tok * top_k
    slot_e = top_e.reshape(n_slot)
    slot_w = gate_w.reshape(n_slot)
    order = jnp.argsort(slot_e, stable=True).astype(i32)
    inv = jnp.argsort(order).astype(i32)
    experts = jnp.arange(n_experts, dtype=i32)
    padded = (counts + blk - 1) // blk * blk
    pad_end = jnp.cumsum(padded)
    pad_start = pad_end - padded
    start = jnp.cumsum(counts) - counts
    shift = pad_start - start
    n_blocks = -(-n_slot // blk) + n_experts
    n_rows = n_blocks * blk
    n_used = pad_end[-1] // blk
    last_e = jnp.max(jnp.where(counts > 0, experts, 0))
    block_start = jnp.arange(n_blocks, dtype=i32) * blk
    block_e = jnp.minimum(jnp.sum((pad_end[None, :] <= block_start[:, None]).astype(i32), axis=1), last_e)
    row = (block_start[:, None] + jnp.arange(blk, dtype=i32)[None, :])
    in_expert = row - pad_start[block_e][:, None]
    valid = (in_expert < counts[block_e][:, None]) & (row < pad_end[-1])
    row_slot = order[jnp.where(valid, row - shift[block_e][:, None], 0).reshape(n_rows)]
    valid = valid.reshape(n_rows)
    row_tok = jnp.where(valid, row_slot // top_k, 0)
    row_w = jnp.where(valid, slot_w[row_slot], 0.0)
    pos = inv + shift[slot_e]
    live = jnp.where(counts > 0, experts, n_experts)
    later = jnp.concatenate([lax.cummin(live[::-1])[::-1][1:], jnp.full((1,), n_experts, i32)])
    next_live = jnp.where(later < n_experts, later, -1)
    prev_e = jnp.concatenate([jnp.full((1,), -1, i32), block_e[:-1]])
    run_first = ((block_e != prev_e) & (block_start < pad_end[-1])).astype(i32)
    misc = jnp.stack([n_used, jnp.min(live)]).astype(i32)
    return row_tok, row_w.reshape(n_rows, 1), (block_e, run_first, next_live[block_e], misc), pos


def _gather_kernel(tok_ref, nu_ref, hg_hbm, o_ref, x3a_ref, x3b_ref, sem, *, blk, n_sub, unroll):
    b = pl.program_id(0)
    n_used = nu_ref[0]
    pitch = n_sub + SLAB_PAD
    bufs = (x3a_ref, x3b_ref)

    def issue(block_idx, slot):
        def body(r0, carry):
            for q in range(unroll):
                r = r0 * unroll + q
                tok = tok_ref[block_idx * blk + r]
                pltpu.make_async_copy(hg_hbm.at[pl.ds(tok * pitch, n_sub)], bufs[slot].at[:, r, :],
                                      sem.at[slot]).start(priority=q % 2)
            return carry
        lax.fori_loop(0, blk // unroll, body, 0)

    def wait_all(slot):
        pltpu.make_async_copy(bufs[slot], bufs[slot], sem.at[slot]).wait()

    @pl.when(b < n_used)
    def _():
        @pl.when(b == 0)
        def _():
            issue(0, 0)
        for slot in range(2):
            @pl.when(b % 2 == slot)
            def _():
                @pl.when(b + 1 < n_used)
                def _():
                    issue(b + 1, 1 - slot)
                wait_all(slot)
                for s in range(n_sub):
                    lo, hi = _unpack_bf16_pair(bufs[slot][s])
                    o_ref[:, s * LANES:(s + 1) * LANES] = lo.astype(o_ref.dtype)
                    o_ref[:, (n_sub + s) * LANES:(n_sub + s + 1) * LANES] = hi.astype(o_ref.dtype)

    @pl.when(b >= n_used)
    def _():
        o_ref[...] = jnp.zeros(o_ref.shape, o_ref.dtype)


def _stream_expert_weights(b, f, nf, be_ref, first_ref, nxt_ref, first_e, copies, stages, works):
    @pl.when(first_ref[b] == 1)
    def _():
        e = be_ref[b]

        @pl.when((f == 0) & (b == 0))
        def _():
            for copy in copies:
                copy(e, f).start()
        for copy in copies:
            copy(e, f).wait()
        for stage, work in zip(stages, works):
            work[...] = stage[...].astype(work.dtype)
        more = nxt_ref[b] >= 0

        @pl.when(more | (f + 1 < nf))
        def _():
            e2 = jnp.where(more, nxt_ref[b], first_e)
            f2 = jnp.where(more, f, f + 1)
            for copy in copies:
                copy(e2, f2).start()


def _gate_up_kernel(be_ref, first_ref, nxt_ref, misc_ref, x_ref, wg_hbm, bg_ref, wu_hbm, bu_ref, o_ref,
                    sg_ref, su_ref, wg_ref, wu_ref, sem, *, tf, nf):
    f = pl.program_id(0)
    b = pl.program_id(1)
    n_used, first_e = misc_ref[0], misc_ref[1]

    def tile_copy(hbm, stage, k):
        return lambda e, fi: pltpu.make_async_copy(
            hbm.at[e, :, pl.ds(pl.multiple_of(fi * tf, tf), tf)], stage, sem.at[k])

    @pl.when(b < n_used)
    def _():
        _stream_expert_weights(b, f, nf, be_ref, first_ref, nxt_ref, first_e,
                               (tile_copy(wg_hbm, sg_ref, 0), tile_copy(wu_hbm, su_ref, 1)),
                               (sg_ref, su_ref), (wg_ref, wu_ref))
        x = x_ref[...]
        g = jnp.minimum(jnp.dot(x, wg_ref[...], preferred_element_type=f32) + bg_ref[...], SWIGLU_LIMIT)
        up = jnp.clip(jnp.dot(x, wu_ref[...], preferred_element_type=f32) + bu_ref[...],
                      -SWIGLU_LIMIT, SWIGLU_LIMIT)
        o_ref[...] = ((up + 1.0) * (g * jax.nn.sigmoid(SWIGLU_ALPHA * g))).astype(o_ref.dtype)

    @pl.when(b >= n_used)
    def _():
        o_ref[...] = jnp.zeros(o_ref.shape, o_ref.dtype)


def _down_kernel(be_ref, first_ref, nxt_ref, misc_ref, a_ref, wd_hbm, bd_ref, rw_ref, o_ref,
                 sd_ref, wd_ref, sem, *, blk, n_sub, tc):
    b = pl.program_id(0)
    half = n_sub * LANES
    n_used, first_e = misc_ref[0], misc_ref[1]

    @pl.when(b < n_used)
    def _():
        copy = lambda e, fi: pltpu.make_async_copy(wd_hbm.at[e], sd_ref, sem.at[0])
        _stream_expert_weights(b, 0, 1, be_ref, first_ref, nxt_ref, first_e, (copy,), (sd_ref,), (wd_ref,))
        a = a_ref[...]
        rw = rw_ref[...]
        for c in range(half // tc):
            ys = []
            for c0 in (c * tc, half + c * tc):
                cs = slice(c0, c0 + tc)
                ys.append((jnp.dot(a, wd_ref[:, cs], preferred_element_type=f32) + bd_ref[:, cs]) * rw)
            _store_token_chunks(o_ref, ys[0], ys[1], blk, n_sub, c * tc // LANES)
        _zero_spare_rows(o_ref, blk, n_sub)

    @pl.when(b >= n_used)
    def _():
        o_ref[...] = jnp.zeros(o_ref.shape, o_ref.dtype)


def _experts(hg, row_tok, row_w, runs, wg, bg, wu, bu, wd, bd, tiles):
    block_e, run_first, run_next, misc = runs
    n_e, d, ff = wg.shape
    n_sub = d // (2 * LANES)
    pitch = n_sub + SLAB_PAD
    blk = tiles.expert_rows
    n_rows = row_tok.shape[0]
    n_blocks = n_rows // blk
    tf = min(tiles.expert_cols, ff)
    nf = ff // tf
    xs = pl.pallas_call(
        functools.partial(_gather_kernel, blk=blk, n_sub=n_sub, unroll=math.gcd(blk, tiles.gather_unroll)),
        grid_spec=pltpu.PrefetchScalarGridSpec(
            num_scalar_prefetch=2,
            grid=(n_blocks,),
            in_specs=[pl.BlockSpec(memory_space=pl.ANY)],
            out_specs=pl.BlockSpec((blk, d), lambda b, tok, nu: (b, 0)),
            scratch_shapes=[pltpu.VMEM((n_sub, blk, LANES), jnp.uint32),
                            pltpu.VMEM((n_sub, blk, LANES), jnp.uint32),
                            pltpu.SemaphoreType.DMA((2,))]),
        out_shape=jax.ShapeDtypeStruct((n_rows, d), bf16),
        compiler_params=_params("arbitrary"),
        name="expert_gather",
    )(row_tok, misc, hg)
    xb = lambda f, b, be, fi, nx, ms: (jnp.minimum(b, ms[0] - 1), 0)
    bias = lambda f, b, be, fi, nx, ms: (be[b], 0, f)
    act = pl.pallas_call(
        functools.partial(_gate_up_kernel, tf=tf, nf=nf),
        grid_spec=pltpu.PrefetchScalarGridSpec(
            num_scalar_prefetch=4,
            grid=(nf, n_blocks),
            in_specs=[pl.BlockSpec((blk, d), xb),
                      pl.BlockSpec(memory_space=pl.ANY),
                      pl.BlockSpec((None, 1, tf), bias),
                      pl.BlockSpec(memory_space=pl.ANY),
                      pl.BlockSpec((None, 1, tf), bias)],
            out_specs=pl.BlockSpec((blk, tf), lambda f, b, be, fi, nx, ms: (b, f)),
            scratch_shapes=[pltpu.VMEM((d, tf), f32), pltpu.VMEM((d, tf), f32),
                            pltpu.VMEM((d, tf), bf16), pltpu.VMEM((d, tf), bf16),
                            pltpu.SemaphoreType.DMA((2,))]),
        out_shape=jax.ShapeDtypeStruct((n_rows, ff), bf16),
        compiler_params=_params("arbitrary", "arbitrary"),
        name="expert_gate_up",
    )(block_e, run_first, run_next, misc, xs, wg, bg.reshape(n_e, 1, ff).astype(f32), wu,
      bu.reshape(n_e, 1, ff).astype(f32))
    tc = min(tiles.down_cols, d)
    return pl.pallas_call(
        functools.partial(_down_kernel, blk=blk, n_sub=n_sub, tc=tc),
        grid_spec=pltpu.PrefetchScalarGridSpec(
            num_scalar_prefetch=4,
            grid=(n_blocks,),
            in_specs=[pl.BlockSpec((blk, ff), lambda b, be, fi, nx, ms: (jnp.minimum(b, ms[0] - 1), 0)),
                      pl.BlockSpec(memory_space=pl.ANY),
                      pl.BlockSpec((None, 1, d), lambda b, be, fi, nx, ms: (be[b], 0, 0)),
                      pl.BlockSpec((blk, 1), lambda b, be, fi, nx, ms: (b, 0))],
            out_specs=pl.BlockSpec((blk * pitch, LANES), lambda b, be, fi, nx, ms: (b, 0)),
            scratch_shapes=[pltpu.VMEM((ff, d), f32), pltpu.VMEM((ff, d), bf16),
                            pltpu.SemaphoreType.DMA((1,))]),
        out_shape=jax.ShapeDtypeStruct((n_rows * pitch, LANES), jnp.uint32),
        compiler_params=_params("arbitrary"),
        name="expert_down",
    )(block_e, run_first, run_next, misc, act, wd, bd.reshape(n_e, 1, d).astype(f32), row_w)


def _combine_kernel(pos_ref, x_ref, ys_hbm, g_ref, o_ref, b0_ref, b1_ref, sem, *, tm, n_sub, top_k,
                    final_norm):
    i = pl.program_id(0)
    nb = pl.num_programs(0)
    bufs = (b0_ref, b1_ref)
    pitch = n_sub + SLAB_PAD

    def issue(block_idx, slot):
        def body(t0, carry):
            for q in range(2):
                t = t0 * 2 + q
                for k in range(top_k):
                    row = pos_ref[(block_idx * tm + t) * top_k + k]
                    pltpu.make_async_copy(ys_hbm.at[pl.ds(row * pitch, n_sub)],
                                          bufs[slot].at[:, k * tm + t, :], sem.at[slot]).start(priority=k % 2)
            return carry
        lax.fori_loop(0, tm // 2, body, 0)

    def wait_all(slot):
        pltpu.make_async_copy(bufs[slot], bufs[slot], sem.at[slot]).wait()

    @pl.when(i == 0)
    def _():
        issue(0, 0)

    for slot in range(2):
        @pl.when(i % 2 == slot)
        def _():
            @pl.when(i + 1 < nb)
            def _():
                issue(i + 1, 1 - slot)
            wait_all(slot)
            for s in range(n_sub):
                cs_lo = slice(s * LANES, (s + 1) * LANES)
                cs_hi = slice((n_sub + s) * LANES, (n_sub + s + 1) * LANES)
                acc_lo = x_ref[:, cs_lo]
                acc_hi = x_ref[:, cs_hi]
                for k in range(top_k):
                    lo, hi = _unpack_bf16_pair(bufs[slot][s, k * tm:(k + 1) * tm, :])
                    acc_lo = acc_lo + lo
                    acc_hi = acc_hi + hi
                o_ref[:, cs_lo] = acc_lo
                o_ref[:, cs_hi] = acc_hi

    if final_norm:
        x = o_ref[...]
        ms = jnp.mean(x * x, axis=-1, keepdims=True)
        o_ref[...] = x * lax.rsqrt(ms + RMS_EPS) * g_ref[...]


def _combine(x2d, ys, pos, g_final, final_norm, cfg, tiles):
    n, d = x2d.shape
    n_sub = d // (2 * LANES)
    tm = min(tiles.combine_rows, n)
    top_k = cfg.top_k
    grid_spec = pltpu.PrefetchScalarGridSpec(
        num_scalar_prefetch=1,
        grid=(n // tm,),
        in_specs=[pl.BlockSpec((tm, d), lambda i, pos: (i, 0)),
                  pl.BlockSpec(memory_space=pl.ANY),
                  pl.BlockSpec((1, d), lambda i, pos: (0, 0))],
        out_specs=pl.BlockSpec((tm, d), lambda i, pos: (i, 0)),
        scratch_shapes=[pltpu.VMEM((n_sub, top_k * tm, LANES), jnp.uint32),
                        pltpu.VMEM((n_sub, top_k * tm, LANES), jnp.uint32),
                        pltpu.SemaphoreType.DMA((2,))],
    )
    return pl.pallas_call(
        functools.partial(_combine_kernel, tm=tm, n_sub=n_sub, top_k=top_k, final_norm=final_norm),
        grid_spec=grid_spec,
        out_shape=jax.ShapeDtypeStruct((n, d), f32),
        compiler_params=_params("arbitrary"),
        name="moe_combine",
    )(pos, x2d, ys, g_final.reshape(1, d).astype(f32))


def _layer(x, mem, p, l, is_last, g_final, cfg, tiles):
    b, s, d = x.shape
    n = b * s
    m_len = mem.shape[1]
    diff_width = d // 2
    conv_ch = d - diff_width
    x2d = x.reshape(n, d)

    h = _rmsnorm(x2d, p['g_mix'][l], tiles.norm_rows)
    head_dim = p['lambda_q1'].shape[-1]
    in_cols = p['w_in'].shape[-1]
    q_scale = jnp.where(jnp.arange(in_cols) < diff_width, head_dim ** -0.5 * LOG2E, 1.0)
    proj = _matmul([h], p['w_in'][l], None, bf16, tiles.mm_rows, tiles.mm_cols, "in_proj",
                   col_scale=q_scale)
    proj3 = proj.reshape(b, s, -1)
    lam_init = 0.8 - 0.6 * math.exp(-0.3 * l)
    a_out = _diff_attention(proj3, p['lambda_q1'][l], p['lambda_k1'][l], p['lambda_q2'][l],
                            p['lambda_k2'][l], p['g_subln'][l], lam_init, cfg, tiles)
    c_out = _conformer(proj3, 3 * diff_width, p['w_dw'][l], p['b_dw'][l], p['g_conv_ln'][l],
                       p['b_conv_ln'][l], p['w_conv_pw'][l].astype(bf16), cfg, tiles)
    assert diff_width == conv_ch
    x2d = _matmul([a_out.reshape(n, diff_width), c_out.reshape(n, conv_ch)],
                  p['w_out'][l], x2d, f32, tiles.mm_rows, tiles.mm_cols, "out_proj")

    hc = _rmsnorm(x2d, p['g_cross'][l], tiles.norm_rows)
    mem_n = _rmsnorm(mem.reshape(b * m_len, d), p['g_mem'][l], tiles.norm_rows)
    qc = _matmul([hc], p['w_cq'][l], None, bf16, tiles.mm_rows, tiles.mm_cols, "xq_proj")
    kc = _matmul([mem_n], p['w_ck'][l], None, bf16, tiles.mm_rows, tiles.mm_cols, "xk_proj")
    vc = _matmul([mem_n], p['w_cv'][l], None, bf16, tiles.mm_rows, tiles.mm_cols, "xv_proj")
    oc = _cross_attention(qc.reshape(b, s, d), kc.reshape(b, m_len, d), vc.reshape(b, m_len, d),
                          cfg, tiles)
    x2d = _matmul([oc.reshape(n, d)], p['w_co'][l], x2d, f32, tiles.mm_rows, tiles.mm_cols, "xo_proj")

    hg, top_e, gate_w, counts = _norm_router(x2d, p['g_ffn'][l], p['w_router'][l], p['b_router'][l],
                                             cfg, tiles)
    row_tok, row_w, runs, pos = _routing_plan(top_e, gate_w, counts, tiles.expert_rows)
    ys = _experts(hg, row_tok, row_w, runs, p['w_gate'][l], p['b_gate'][l], p['w_up'][l], p['b_up'][l],
                  p['w_down'][l], p['b_down'][l], tiles)
    x2d = _combine(x2d, ys, pos, g_final, is_last, cfg, tiles)
    return x2d.reshape(b, s, d)


def _forward(x, mem, params, g_final, cfg, tiles):
    depth = params['g_mix'].shape[0]
    for l in range(depth):
        x = _layer(x, mem, params, l, l == depth - 1, g_final, cfg, tiles)
    return x


def kernel(x, mem, g_mix, w_in, lambda_q1, lambda_k1, lambda_q2, lambda_k2, g_subln, w_dw, b_dw,
           g_conv_ln, b_conv_ln, w_conv_pw, w_out, g_cross, g_mem, w_cq, w_ck, w_cv, w_co, g_ffn,
           w_router, b_router, w_gate, b_gate, w_up, b_up, w_down, b_down, g_final):
    params = dict(g_mix=g_mix, w_in=w_in, lambda_q1=lambda_q1, lambda_k1=lambda_k1,
                  lambda_q2=lambda_q2, lambda_k2=lambda_k2, g_subln=g_subln, w_dw=w_dw, b_dw=b_dw,
                  g_conv_ln=g_conv_ln, b_conv_ln=b_conv_ln, w_conv_pw=w_conv_pw, w_out=w_out,
                  g_cross=g_cross, g_mem=g_mem, w_cq=w_cq, w_ck=w_ck, w_cv=w_cv, w_co=w_co,
                  g_ffn=g_ffn, w_router=w_router, b_router=b_router, w_gate=w_gate, b_gate=b_gate,
                  w_up=w_up, b_up=b_up, w_down=w_down, b_down=b_down)
    return _forward(x, mem, params, g_final, Cfg(), Tiles())
```

```python
import functools
import math
from typing import NamedTuple

import jax
import jax.numpy as jnp
from jax import lax
from jax.experimental import pallas as pl
from jax.experimental.pallas import tpu as pltpu

f32 = jnp.float32
bf16 = jnp.bfloat16

RMS_EPS = 1e-6
LN_EPS = 1e-5
NEG_INF = -1e30
SWIGLU_LIMIT = 7.0
SWIGLU_ALPHA = 1.702
LOG2E = 1.4426950408889634

LANES = 128
SUBLANES = 8
SLAB_PAD = 1
VMEM_LIMIT_BYTES = 56 * 1024 * 1024


class Cfg(NamedTuple):
    diff_heads: int = 8
    conv_width: int = 31
    x_heads: int = 4
    top_k: int = 4


class Tiles(NamedTuple):
    norm_rows: int = 256
    mm_rows: int = 1024
    mm_cols: int = 512
    attn_q: int = 1024
    conv_rows: int = 256
    conv_halo: int = 32
    conv_acc_rows: int = 128
    xattn_q: int = 512
    router_rows: int = 256
    expert_rows: int = 256
    gather_unroll: int = 8
    expert_cols: int = 768
    down_cols: int = 512
    combine_rows: int = 128


def _params(*sem):
    return pltpu.CompilerParams(dimension_semantics=sem, vmem_limit_bytes=VMEM_LIMIT_BYTES)


def _rmsnorm_kernel(x_ref, g_ref, o_ref):
    x = x_ref[...]
    ms = jnp.mean(x * x, axis=-1, keepdims=True)
    o_ref[...] = (x * lax.rsqrt(ms + RMS_EPS) * g_ref[...]).astype(o_ref.dtype)


def _rmsnorm(x2d, g, rows):
    n, d = x2d.shape
    rows = min(rows, n)
    return pl.pallas_call(
        _rmsnorm_kernel,
        grid=(n // rows,),
        in_specs=[pl.BlockSpec((rows, d), lambda i: (i, 0)),
                  pl.BlockSpec((1, d), lambda i: (0, 0))],
        out_specs=pl.BlockSpec((rows, d), lambda i: (i, 0)),
        out_shape=jax.ShapeDtypeStruct((n, d), bf16),
        compiler_params=_params("parallel"),
        name="rmsnorm",
    )(x2d, g.reshape(1, d).astype(f32))


def _matmul_kernel(*refs, n_lhs, has_res, has_scale):
    a_refs = refs[:n_lhs]
    w_refs = refs[n_lhs:2 * n_lhs]
    extra = list(refs[2 * n_lhs:-1 - n_lhs])
    o_ref = refs[-1 - n_lhs]
    wb_refs = refs[len(refs) - n_lhs:]

    @pl.when(pl.program_id(1) == 0)
    def _():
        for w_ref, wb_ref in zip(w_refs, wb_refs):
            wb_ref[...] = w_ref[...].astype(bf16)

    acc = jnp.dot(a_refs[0][...], wb_refs[0][...], preferred_element_type=f32)
    for p in range(1, n_lhs):
        acc = acc + jnp.dot(a_refs[p][...], wb_refs[p][...], preferred_element_type=f32)
    if has_res:
        acc = acc + extra.pop(0)[...]
    if has_scale:
        acc = acc * extra.pop(0)[...]
    o_ref[...] = acc.astype(o_ref.dtype)


def _matmul(lhs_list, w, res, out_dtype, rows, cols, name, col_scale=None):
    n_lhs = len(lhs_list)
    m, kp = lhs_list[0].shape
    nc = w.shape[1]
    assert w.shape[0] == n_lhs * kp
    rows = min(rows, m)
    cols = min(cols, nc)
    in_specs = [pl.BlockSpec((rows, kp), lambda j, i: (i, 0)) for _ in range(n_lhs)]
    in_specs += [pl.BlockSpec((kp, cols), lambda j, i, p=p: (p, j)) for p in range(n_lhs)]
    args = list(lhs_list) + [w] * n_lhs
    if res is not None:
        in_specs.append(pl.BlockSpec((rows, cols), lambda j, i: (i, j)))
        args.append(res)
    if col_scale is not None:
        in_specs.append(pl.BlockSpec((1, cols), lambda j, i: (0, j)))
        args.append(col_scale.reshape(1, nc).astype(f32))
    return pl.pallas_call(
        functools.partial(_matmul_kernel, n_lhs=n_lhs, has_res=res is not None,
                          has_scale=col_scale is not None),
        grid=(nc // cols, m // rows),
        in_specs=in_specs,
        out_specs=pl.BlockSpec((rows, cols), lambda j, i: (i, j)),
        out_shape=jax.ShapeDtypeStruct((m, nc), out_dtype),
        scratch_shapes=[pltpu.VMEM((kp, cols), bf16) for _ in range(n_lhs)],
        compiler_params=_params("parallel", "arbitrary"),
        name=name,
    )(*args)


def _bf16_part(x):
    bits = lax.bitcast_convert_type(x, jnp.uint32) & jnp.uint32(0xFFFF0000)
    return lax.bitcast_convert_type(bits, f32)


def _lane_tile(x, n):
    return x if n == 1 else jnp.concatenate([x] * n, axis=1)


def _diff_attn_kernel(slopes_ref, q_ref, k_ref, v_ref, lq1_ref, lk1_ref, lq2_ref, lk2_ref, g_ref,
                      o_ref, qa1_ref, qa2_ref, ka1_ref, ka2_ref, acc1_ref, acc2_ref,
                      m1_ref, l1_ref, m2_ref, l2_ref, *, tq, d, lam_init):
    h = pl.program_id(1)
    qi = pl.program_id(2)
    tk = tq
    slope2 = slopes_ref[h]
    lane_q = lax.broadcasted_iota(jnp.int32, (tq, d), 1)
    ones_cols = jnp.where(lane_q < 3, 1.0, 0.0).astype(bf16)
    q = q_ref[...]
    qa1_ref[:, :d] = q[:, :d]
    qa1_ref[:, d:] = ones_cols
    qa2_ref[:, :d] = q[:, d:]
    qa2_ref[:, d:] = ones_cols
    lane_k = lax.broadcasted_iota(jnp.int32, (tk, d), 1)
    v0 = slope2 * lax.broadcasted_iota(jnp.int32, (tk, d), 0).astype(f32)
    hi = _bf16_part(v0)
    r1 = v0 - hi
    lo = _bf16_part(r1)
    lo2 = r1 - lo
    bias_cols = jnp.where(lane_k == 0, hi, jnp.where(lane_k == 1, lo, jnp.where(lane_k == 2, lo2, 0.0)))
    ka1_ref[:, d:] = bias_cols.astype(bf16)
    ka2_ref[:, d:] = bias_cols.astype(bf16)
    rel = (lax.broadcasted_iota(jnp.int32, (tq, tk), 0)
           - lax.broadcasted_iota(jnp.int32, (tq, tk), 1))
    maps = ((qa1_ref, ka1_ref, m1_ref, l1_ref, acc1_ref), (qa2_ref, ka2_ref, m2_ref, l2_ref, acc2_ref))
    for _, _, m_ref, l_ref, acc_ref in maps:
        m_ref[...] = jnp.full(m_ref.shape, NEG_INF, f32)
        l_ref[...] = jnp.zeros(l_ref.shape, f32)
        acc_ref[...] = jnp.zeros(acc_ref.shape, f32)

    def block(j, masked):
        k0 = pl.multiple_of(j * tk, tk)
        kb = k_ref[pl.ds(k0, tk), :]
        vb = v_ref[pl.ds(k0, tk), :]
        cj = slope2 * ((j - qi) * tk).astype(f32)
        for c, (qa_ref, ka_ref, m_ref, l_ref, acc_ref) in enumerate(maps):
            ka_ref[:, :d] = kb[:, c * d:(c + 1) * d]
            s = lax.dot_general(qa_ref[...], ka_ref[...], (((1,), (1,)), ((), ())),
                                preferred_element_type=f32)
            if masked:
                s = jnp.where(rel >= 0, s, NEG_INF)
            m_old = m_ref[...]
            m_new = jnp.maximum(m_old, jnp.max(s, axis=-1, keepdims=True) + cj)
            p = jnp.exp2(s - _lane_tile(m_new - cj, tk // LANES))
            alpha = jnp.exp2(m_old - m_new)
            l_ref[...] = alpha * l_ref[...] + jnp.sum(p, axis=-1, keepdims=True)
            acc_ref[...] = (_lane_tile(alpha, acc_ref.shape[1] // LANES) * acc_ref[...]
                            + jnp.dot(p.astype(bf16), vb, preferred_element_type=f32))
            m_ref[...] = m_new

    def off_diag(j, carry):
        block(j, False)
        return carry

    lax.fori_loop(0, qi, off_diag, 0)
    block(qi, True)

    lam = (jnp.exp(jnp.sum(lq1_ref[...] * lk1_ref[...], axis=-1, keepdims=True))
           - jnp.exp(jnp.sum(lq2_ref[...] * lk2_ref[...], axis=-1, keepdims=True)) + lam_init)
    reps = acc1_ref.shape[1] // LANES
    o = (acc1_ref[...] / _lane_tile(l1_ref[...], reps)
         - lam * (acc2_ref[...] / _lane_tile(l2_ref[...], reps)))
    ms = jnp.mean(o * o, axis=-1, keepdims=True)
    y = o * lax.rsqrt(ms + RMS_EPS) * g_ref[...] * (1.0 - lam_init)
    o_ref[...] = y.astype(o_ref.dtype)


def _diff_attention(proj3, lq1, lk1, lq2, lk2, g_subln, lam_init, cfg, tiles):
    b, s, _ = proj3.shape
    nh = cfg.diff_heads
    d = lq1.shape[-1]
    hw = 2 * d
    tq = min(tiles.attn_q, s)
    slopes = jnp.exp2(-8.0 * jnp.arange(1, nh + 1, dtype=f32) / nh) * LOG2E
    vec = lambda a: a.reshape(1, -1).astype(f32)
    small = lambda n: pl.BlockSpec((1, n), lambda bi, hi, qi: (0, 0))
    return pl.pallas_call(
        functools.partial(_diff_attn_kernel, tq=tq, d=d, lam_init=lam_init),
        grid=(b, nh, s // tq),
        in_specs=[pl.BlockSpec(memory_space=pltpu.SMEM),
                  pl.BlockSpec((None, tq, hw), lambda bi, hi, qi: (bi, qi, hi)),
                  pl.BlockSpec((None, s, hw), lambda bi, hi, qi: (bi, 0, nh + hi)),
                  pl.BlockSpec((None, s, hw), lambda bi, hi, qi: (bi, 0, 2 * nh + hi)),
                  small(d), small(d), small(d), small(d), small(hw)],
        out_specs=pl.BlockSpec((None, tq, hw), lambda bi, hi, qi: (bi, qi, hi)),
        out_shape=jax.ShapeDtypeStruct((b, s, nh * hw), bf16),
        scratch_shapes=[pltpu.VMEM((tq, hw), bf16), pltpu.VMEM((tq, hw), bf16),
                        pltpu.VMEM((tq, hw), bf16), pltpu.VMEM((tq, hw), bf16),
                        pltpu.VMEM((tq, hw), f32), pltpu.VMEM((tq, hw), f32),
                        pltpu.VMEM((tq, LANES), f32), pltpu.VMEM((tq, LANES), f32),
                        pltpu.VMEM((tq, LANES), f32), pltpu.VMEM((tq, LANES), f32)],
        compiler_params=_params("parallel", "parallel", "parallel"),
        name="diff_attention",
    )(slopes, proj3, proj3, proj3, vec(lq1), vec(lk1), vec(lq2), vec(lk2), vec(g_subln))


def _conformer_kernel(a_ref, gate_ref, ah_ref, gateh_ref, wdw_ref, bdw_ref, gln_ref, bln_ref,
                      wpw_ref, o_ref, buf_ref, conv_ref, win_ref, *, ts, kw, halo, rt):
    i = pl.program_id(1)
    ch = a_ref.shape[-1]
    glu = a_ref[...].astype(f32) * jax.nn.sigmoid(gate_ref[...].astype(f32))
    glu_h = ah_ref[...].astype(f32) * jax.nn.sigmoid(gateh_ref[...].astype(f32))
    buf_ref[0:halo, :] = jnp.where(i > 0, glu_h, 0.0)
    buf_ref[halo:halo + ts, :] = glu
    off = halo - (kw - 1)
    def conv_chunk(c, carry):
        cs = pl.ds(pl.multiple_of(c * LANES, LANES), LANES)
        for r0 in range(0, ts, rt):
            acc = jnp.broadcast_to(bdw_ref[:, cs], (rt, LANES))
            for r in range(min(SUBLANES, kw)):
                span = (kw - 1 - r) // SUBLANES * SUBLANES
                win_ref[r, 0:rt + span, :] = buf_ref[r0 + off + r:r0 + off + r + rt + span, cs]
                for j in range(r, kw, SUBLANES):
                    acc = acc + wdw_ref[j:j + 1, cs] * win_ref[r, j - r:j - r + rt, :]
            conv_ref[r0:r0 + rt, cs] = acc
        return carry

    lax.fori_loop(0, ch // LANES, conv_chunk, 0)
    c = conv_ref[...]
    mu = jnp.mean(c, axis=-1, keepdims=True)
    xc = c - mu
    var = jnp.mean(xc * xc, axis=-1, keepdims=True)
    y = xc * lax.rsqrt(var + LN_EPS) * gln_ref[...] + bln_ref[...]
    act = y * jax.nn.sigmoid(y)
    o_ref[...] = jnp.dot(act.astype(bf16), wpw_ref[...],
                         preferred_element_type=f32).astype(o_ref.dtype)


def _conformer(proj3, col0, w_dw, b_dw, g_ln, b_ln, w_pw, cfg, tiles):
    b, s, _ = proj3.shape
    kw = cfg.conv_width
    ch = w_pw.shape[0]
    ts = min(tiles.conv_rows, s)
    halo = tiles.conv_halo
    assert halo >= kw - 1 and ts % halo == 0 and col0 % ch == 0
    cb = col0 // ch
    hb = ts // halo
    rt = math.gcd(ts, tiles.conv_acc_rows)
    win_rows = rt + (kw - 1) // SUBLANES * SUBLANES
    vec = lambda a: a.reshape(1, ch).astype(f32)
    small = lambda: pl.BlockSpec((1, ch), lambda bi, i: (0, 0))
    return pl.pallas_call(
        functools.partial(_conformer_kernel, ts=ts, kw=kw, halo=halo, rt=rt),
        grid=(b, s // ts),
        in_specs=[pl.BlockSpec((None, ts, ch), lambda bi, i: (bi, i, cb)),
                  pl.BlockSpec((None, ts, ch), lambda bi, i: (bi, i, cb + 1)),
                  pl.BlockSpec((None, halo, ch), lambda bi, i: (bi, jnp.maximum(i * hb - 1, 0), cb)),
                  pl.BlockSpec((None, halo, ch), lambda bi, i: (bi, jnp.maximum(i * hb - 1, 0), cb + 1)),
                  pl.BlockSpec((kw, ch), lambda bi, i: (0, 0)),
                  small(), small(), small(),
                  pl.BlockSpec((ch, ch), lambda bi, i: (0, 0))],
        out_specs=pl.BlockSpec((None, ts, ch), lambda bi, i: (bi, i, 0)),
        out_shape=jax.ShapeDtypeStruct((b, s, ch), bf16),
        scratch_shapes=[pltpu.VMEM((halo + ts, ch), f32), pltpu.VMEM((ts, ch), f32),
                        pltpu.VMEM((SUBLANES, win_rows, LANES), f32)],
        compiler_params=_params("parallel", "parallel"),
        name="conformer",
    )(proj3, proj3, proj3, proj3, w_dw.reshape(kw, ch).astype(f32), vec(b_dw), vec(g_ln), vec(b_ln),
      w_pw)


def _xattn_kernel(q_ref, k_ref, v_ref, o_ref, *, scale):
    s = lax.dot_general(q_ref[...], k_ref[...], (((1,), (1,)), ((), ())),
                        preferred_element_type=f32) * scale
    m = jnp.max(s, axis=-1, keepdims=True)
    p = jnp.exp(s - m)
    p = p / jnp.sum(p, axis=-1, keepdims=True)
    o_ref[...] = jnp.dot(p.astype(bf16), v_ref[...], preferred_element_type=f32).astype(o_ref.dtype)


def _cross_attention(q3, k3, v3, cfg, tiles):
    b, s, dm = q3.shape
    m = k3.shape[1]
    nh = cfg.x_heads
    hd = dm // nh
    tq = min(tiles.xattn_q, s)
    return pl.pallas_call(
        functools.partial(_xattn_kernel, scale=hd ** -0.5),
        grid=(b, s // tq, nh),
        in_specs=[pl.BlockSpec((None, tq, hd), lambda bi, i, h: (bi, i, h)),
                  pl.BlockSpec((None, m, hd), lambda bi, i, h: (bi, 0, h)),
                  pl.BlockSpec((None, m, hd), lambda bi, i, h: (bi, 0, h))],
        out_specs=pl.BlockSpec((None, tq, hd), lambda bi, i, h: (bi, i, h)),
        out_shape=jax.ShapeDtypeStruct((b, s, dm), bf16),
        compiler_params=_params("parallel", "parallel", "parallel"),
        name="cross_attention",
    )(q3, k3, v3)


def _pack_bf16_pair(lo, hi):
    def rounded(x):
        bits = lax.bitcast_convert_type(x, jnp.uint32)
        return bits + jnp.uint32(0x7FFF) + ((bits >> 16) & jnp.uint32(1))
    return (rounded(hi) & jnp.uint32(0xFFFF0000)) | (rounded(lo) >> 16)


def _unpack_bf16_pair(w):
    lo = lax.bitcast_convert_type(w << 16, f32)
    hi = lax.bitcast_convert_type(w & jnp.uint32(0xFFFF0000), f32)
    return lo, hi


def _store_token_chunks(o_ref, lo, hi, n_rows, n_sub, s0):
    pitch = n_sub + SLAB_PAD
    for s in range(lo.shape[1] // LANES):
        cs = slice(s * LANES, (s + 1) * LANES)
        o_ref[pl.ds(s0 + s, n_rows, stride=pitch), :] = _pack_bf16_pair(lo[:, cs], hi[:, cs])


def _zero_spare_rows(o_ref, n_rows, n_sub):
    o_ref[pl.ds(n_sub, n_rows, stride=n_sub + SLAB_PAD), :] = jnp.zeros((n_rows, LANES), o_ref.dtype)


def _router_kernel(x_ref, g_ref, wr_ref, br_ref, hg_ref, te_ref, tw_ref, cnt_ref, *, tm, n_sub, top_k):
    x = x_ref[...]
    ms = jnp.mean(x * x, axis=-1, keepdims=True)
    h = x * lax.rsqrt(ms + RMS_EPS) * g_ref[...]
    half = n_sub * LANES
    _store_token_chunks(hg_ref, h[:, :half], h[:, half:], tm, n_sub, 0)
    _zero_spare_rows(hg_ref, tm, n_sub)
    logits = jnp.dot(h.astype(bf16), wr_ref[...], preferred_element_type=f32) + br_ref[...]
    lane = lax.broadcasted_iota(jnp.int32, logits.shape, 1)
    lane_f = lane.astype(f32)
    vals, idxs = [], []
    for _ in range(top_k):
        m = jnp.max(logits, axis=-1, keepdims=True)
        idx = jnp.min(jnp.where(logits == m, lane_f, float(LANES)), axis=-1, keepdims=True)
        vals.append(m)
        idxs.append(idx)
        logits = jnp.where(lane_f == idx, NEG_INF * 2.0, logits)
    exps = [jnp.exp(v - vals[0]) for v in vals]
    denom = exps[0]
    for e in exps[1:]:
        denom = denom + e
    te = jnp.zeros(logits.shape, f32)
    tw = jnp.zeros(logits.shape, f32)
    hits = jnp.zeros(logits.shape, f32)
    for k in range(top_k):
        te = jnp.where(lane == k, idxs[k], te)
        tw = jnp.where(lane == k, exps[k] / denom, tw)
        hits = hits + jnp.where(lane_f == idxs[k], 1.0, 0.0)
    te_ref[...] = te.astype(jnp.int32)
    tw_ref[...] = tw
    tile_counts = jnp.sum(hits, axis=0, keepdims=True)
    row8 = lax.broadcasted_iota(jnp.int32, cnt_ref.shape, 0)
    cnt_ref[...] = jnp.where(row8 == 0, tile_counts, 0.0).astype(jnp.int32)


def _norm_router(x2d, g, w_router, b_router, cfg, tiles):
    n, d = x2d.shape
    e = w_router.shape[1]
    assert e <= LANES and d % (2 * LANES) == 0
    n_sub = d // (2 * LANES)
    pitch = n_sub + SLAB_PAD
    tm = min(tiles.router_rows, n)
    wr = jnp.zeros((d, LANES), bf16).at[:, :e].set(w_router.astype(bf16))
    br = jnp.full((1, LANES), NEG_INF, f32).at[0, :e].set(b_router.astype(f32))
    hg, te, tw, cnt = pl.pallas_call(
        functools.partial(_router_kernel, tm=tm, n_sub=n_sub, top_k=cfg.top_k),
        grid=(n // tm,),
        in_specs=[pl.BlockSpec((tm, d), lambda i: (i, 0)),
                  pl.BlockSpec((1, d), lambda i: (0, 0)),
                  pl.BlockSpec((d, LANES), lambda i: (0, 0)),
                  pl.BlockSpec((1, LANES), lambda i: (0, 0))],
        out_specs=[pl.BlockSpec((tm * pitch, LANES), lambda i: (i, 0)),
                   pl.BlockSpec((tm, LANES), lambda i: (i, 0)),
                   pl.BlockSpec((tm, LANES), lambda i: (i, 0)),
                   pl.BlockSpec((SUBLANES, LANES), lambda i: (i, 0))],
        out_shape=[jax.ShapeDtypeStruct((n * pitch, LANES), jnp.uint32),
                   jax.ShapeDtypeStruct((n, LANES), jnp.int32),
                   jax.ShapeDtypeStruct((n, LANES), f32),
                   jax.ShapeDtypeStruct((n // tm * SUBLANES, LANES), jnp.int32)],
        compiler_params=_params("parallel"),
        name="norm_router",
    )(x2d, g.reshape(1, d).astype(f32), wr, br)
    counts = jnp.sum(cnt, axis=0)[:e]
    return hg, te[:, :cfg.top_k], tw[:, :cfg.top_k], counts


def _routing_plan(top_e, gate_w, counts, blk):
    i32 = jnp.int32
    n_tok, top_k = top_e.shape
    n_experts = counts.shape[0]
    n_slot = n_tok * top_k
    slot_e = top_e.reshape(n_slot)
    slot_w = gate_w.reshape(n_slot)
    order = jnp.argsort(slot_e, stable=True).astype(i32)
    inv = jnp.argsort(order).astype(i32)
    experts = jnp.arange(n_experts, dtype=i32)
    padded = (counts + blk - 1) // blk * blk
    pad_end = jnp.cumsum(padded)
    pad_start = pad_end - padded
    start = jnp.cumsum(counts) - counts
    shift = pad_start - start
    n_blocks = -(-n_slot // blk) + n_experts
    n_rows = n_blocks * blk
    n_used = pad_end[-1] // blk
    last_e = jnp.max(jnp.where(counts > 0, experts, 0))
    block_start = jnp.arange(n_blocks, dtype=i32) * blk
    block_e = jnp.minimum(jnp.sum((pad_end[None, :] <= block_start[:, None]).astype(i32), axis=1), last_e)
    row = (block_start[:, None] + jnp.arange(blk, dtype=i32)[None, :])
    in_expert = row - pad_start[block_e][:, None]
    valid = (in_expert < counts[block_e][:, None]) & (row < pad_end[-1])
    row_slot = order[jnp.where(valid, row - shift[block_e][:, None], 0).reshape(n_rows)]
    valid = valid.reshape(n_rows)
    row_tok = jnp.where(valid, row_slot // top_k, 0)
    row_w = jnp.where(valid, slot_w[row_slot], 0.0)
    pos = inv + shift[slot_e]
    live = jnp.where(counts > 0, experts, n_experts)
    later = jnp.concatenate([lax.cummin(live[::-1])[::-1][1:], jnp.full((1,), n_experts, i32)])
    next_live = jnp.where(later < n_experts, later, -1)
    prev_e = jnp.concatenate([jnp.full((1,), -1, i32), block_e[:-1]])
    run_first = ((block_e != prev_e) & (block_start < pad_end[-1])).astype(i32)
    misc = jnp.stack([n_used, jnp.min(live)]).astype(i32)
    return row_tok, row_w.reshape(n_rows, 1), (block_e, run_first, next_live[block_e], misc), pos


def _gather_kernel(tok_ref, nu_ref, hg_hbm, o_ref, x3a_ref, x3b_ref, sem, *, blk, n_sub, unroll):
    b = pl.program_id(0)
    n_used = nu_ref[0]
    pitch = n_sub + SLAB_PAD
    bufs = (x3a_ref, x3b_ref)

    def issue(block_idx, slot):
        def body(r0, carry):
            for q in range(unroll):
                r = r0 * unroll + q
                tok = tok_ref[block_idx * blk + r]
                pltpu.make_async_copy(hg_hbm.at[pl.ds(tok * pitch, n_sub)],
                                      bufs[slot].at[pl.ds(r * pitch, n_sub)],
                                      sem.at[slot]).start(priority=q % 2)
            return carry
        lax.fori_loop(0, blk // unroll, body, 0)

    def wait_all(slot):
        done = bufs[slot].at[pl.ds(0, blk * n_sub)]
        pltpu.make_async_copy(done, done, sem.at[slot]).wait()

    @pl.when(b < n_used)
    def _():
        @pl.when(b == 0)
        def _():
            issue(0, 0)
        for slot in range(2):
            @pl.when(b % 2 == slot)
            def _():
                @pl.when(b + 1 < n_used)
                def _():
                    issue(b + 1, 1 - slot)
                wait_all(slot)
                for s in range(n_sub):
                    lo, hi = _unpack_bf16_pair(bufs[slot][pl.ds(s, blk, stride=pitch), :])
                    o_ref[:, s * LANES:(s + 1) * LANES] = lo.astype(o_ref.dtype)
                    o_ref[:, (n_sub + s) * LANES:(n_sub + s + 1) * LANES] = hi.astype(o_ref.dtype)

    @pl.when(b >= n_used)
    def _():
        o_ref[...] = jnp.zeros(o_ref.shape, o_ref.dtype)


def _stream_expert_weights(b, f, nf, be_ref, first_ref, nxt_ref, first_e, copies, stages, works):
    @pl.when(first_ref[b] == 1)
    def _():
        e = be_ref[b]

        @pl.when((f == 0) & (b == 0))
        def _():
            for copy in copies:
                copy(e, f).start()
        for copy in copies:
            copy(e, f).wait()
        for stage, work in zip(stages, works):
            work[...] = stage[...].astype(work.dtype)
        more = nxt_ref[b] >= 0

        @pl.when(more | (f + 1 < nf))
        def _():
            e2 = jnp.where(more, nxt_ref[b], first_e)
            f2 = jnp.where(more, f, f + 1)
            for copy in copies:
                copy(e2, f2).start()


def _gate_up_kernel(be_ref, first_ref, nxt_ref, misc_ref, x_ref, wg_hbm, bg_ref, wu_hbm, bu_ref, o_ref,
                    sg_ref, su_ref, wg_ref, wu_ref, sem, *, tf, nf):
    f = pl.program_id(0)
    b = pl.program_id(1)
    n_used, first_e = misc_ref[0], misc_ref[1]

    def tile_copy(hbm, stage, k):
        return lambda e, fi: pltpu.make_async_copy(
            hbm.at[e, :, pl.ds(pl.multiple_of(fi * tf, tf), tf)], stage, sem.at[k])

    @pl.when(b < n_used)
    def _():
        _stream_expert_weights(b, f, nf, be_ref, first_ref, nxt_ref, first_e,
                               (tile_copy(wg_hbm, sg_ref, 0), tile_copy(wu_hbm, su_ref, 1)),
                               (sg_ref, su_ref), (wg_ref, wu_ref))
        x = x_ref[...]
        g = jnp.minimum(jnp.dot(x, wg_ref[...], preferred_element_type=f32) + bg_ref[...], SWIGLU_LIMIT)
        up = jnp.clip(jnp.dot(x, wu_ref[...], preferred_element_type=f32) + bu_ref[...],
                      -SWIGLU_LIMIT, SWIGLU_LIMIT)
        o_ref[...] = ((up + 1.0) * (g * jax.nn.sigmoid(SWIGLU_ALPHA * g))).astype(o_ref.dtype)

    @pl.when(b >= n_used)
    def _():
        o_ref[...] = jnp.zeros(o_ref.shape, o_ref.dtype)


def _down_kernel(be_ref, first_ref, nxt_ref, misc_ref, a_ref, wd_hbm, bd_ref, rw_ref, o_ref,
                 sd_ref, wd_ref, sem, *, blk, n_sub, tc):
    b = pl.program_id(0)
    half = n_sub * LANES
    n_used, first_e = misc_ref[0], misc_ref[1]

    @pl.when(b < n_used)
    def _():
        copy = lambda e, fi: pltpu.make_async_copy(wd_hbm.at[e], sd_ref, sem.at[0])
        _stream_expert_weights(b, 0, 1, be_ref, first_ref, nxt_ref, first_e, (copy,), (sd_ref,), (wd_ref,))
        a = a_ref[...]
        rw = rw_ref[...]
        for c in range(half // tc):
            ys = []
            for c0 in (c * tc, half + c * tc):
                cs = slice(c0, c0 + tc)
                ys.append((jnp.dot(a, wd_ref[:, cs], preferred_element_type=f32) + bd_ref[:, cs]) * rw)
            _store_token_chunks(o_ref, ys[0], ys[1], blk, n_sub, c * tc // LANES)
        _zero_spare_rows(o_ref, blk, n_sub)

    @pl.when(b >= n_used)
    def _():
        o_ref[...] = jnp.zeros(o_ref.shape, o_ref.dtype)


def _experts(hg, row_tok, row_w, runs, wg, bg, wu, bu, wd, bd, tiles):
    block_e, run_first, run_next, misc = runs
    n_e, d, ff = wg.shape
    n_sub = d // (2 * LANES)
    pitch = n_sub + SLAB_PAD
    blk = tiles.expert_rows
    n_rows = row_tok.shape[0]
    n_blocks = n_rows // blk
    tf = min(tiles.expert_cols, ff)
    nf = ff // tf
    xs = pl.pallas_call(
        functools.partial(_gather_kernel, blk=blk, n_sub=n_sub, unroll=math.gcd(blk, tiles.gather_unroll)),
        grid_spec=pltpu.PrefetchScalarGridSpec(
            num_scalar_prefetch=2,
            grid=(n_blocks,),
            in_specs=[pl.BlockSpec(memory_space=pl.ANY)],
            out_specs=pl.BlockSpec((blk, d), lambda b, tok, nu: (b, 0)),
            scratch_shapes=[pltpu.VMEM((blk * pitch, LANES), jnp.uint32),
                            pltpu.VMEM((blk * pitch, LANES), jnp.uint32),
                            pltpu.SemaphoreType.DMA((2,))]),
        out_shape=jax.ShapeDtypeStruct((n_rows, d), bf16),
        compiler_params=_params("arbitrary"),
        name="expert_gather",
    )(row_tok, misc, hg)
    xb = lambda f, b, be, fi, nx, ms: (jnp.minimum(b, ms[0] - 1), 0)
    bias = lambda f, b, be, fi, nx, ms: (be[b], 0, f)
    act = pl.pallas_call(
        functools.partial(_gate_up_kernel, tf=tf, nf=nf),
        grid_spec=pltpu.PrefetchScalarGridSpec(
            num_scalar_prefetch=4,
            grid=(nf, n_blocks),
            in_specs=[pl.BlockSpec((blk, d), xb),
                      pl.BlockSpec(memory_space=pl.ANY),
                      pl.BlockSpec((None, 1, tf), bias),
                      pl.BlockSpec(memory_space=pl.ANY),
                      pl.BlockSpec((None, 1, tf), bias)],
            out_specs=pl.BlockSpec((blk, tf), lambda f, b, be, fi, nx, ms: (b, f)),
            scratch_shapes=[pltpu.VMEM((d, tf), f32), pltpu.VMEM((d, tf), f32),
                            pltpu.VMEM((d, tf), bf16), pltpu.VMEM((d, tf), bf16),
                            pltpu.SemaphoreType.DMA((2,))]),
        out_shape=jax.ShapeDtypeStruct((n_rows, ff), bf16),
        compiler_params=_params("arbitrary", "arbitrary"),
        name="expert_gate_up",
    )(block_e, run_first, run_next, misc, xs, wg, bg.reshape(n_e, 1, ff).astype(f32), wu,
      bu.reshape(n_e, 1, ff).astype(f32))
    tc = min(tiles.down_cols, d)
    return pl.pallas_call(
        functools.partial(_down_kernel, blk=blk, n_sub=n_sub, tc=tc),
        grid_spec=pltpu.PrefetchScalarGridSpec(
            num_scalar_prefetch=4,
            grid=(n_blocks,),
            in_specs=[pl.BlockSpec((blk, ff), lambda b, be, fi, nx, ms: (jnp.minimum(b, ms[0] - 1), 0)),
                      pl.BlockSpec(memory_space=pl.ANY),
                      pl.BlockSpec((None, 1, d), lambda b, be, fi, nx, ms: (be[b], 0, 0)),
                      pl.BlockSpec((blk, 1), lambda b, be, fi, nx, ms: (b, 0))],
            out_specs=pl.BlockSpec((blk * pitch, LANES), lambda b, be, fi, nx, ms: (b, 0)),
            scratch_shapes=[pltpu.VMEM((ff, d), f32), pltpu.VMEM((ff, d), bf16),
                            pltpu.SemaphoreType.DMA((1,))]),
        out_shape=jax.ShapeDtypeStruct((n_rows * pitch, LANES), jnp.uint32),
        compiler_params=_params("arbitrary"),
        name="expert_down",
    )(block_e, run_first, run_next, misc, act, wd, bd.reshape(n_e, 1, d).astype(f32), row_w)


def _combine_kernel(pos_ref, x_ref, ys_hbm, g_ref, o_ref, b0_ref, b1_ref, sem, *, tm, n_sub, top_k,
                    final_norm):
    i = pl.program_id(0)
    nb = pl.num_programs(0)
    bufs = (b0_ref, b1_ref)
    pitch = n_sub + SLAB_PAD

    def issue(block_idx, slot):
        def body(t0, carry):
            for q in range(2):
                t = t0 * 2 + q
                for k in range(top_k):
                    row = pos_ref[(block_idx * tm + t) * top_k + k]
                    pltpu.make_async_copy(ys_hbm.at[pl.ds(row * pitch, n_sub)],
                                          bufs[slot].at[pl.ds((k * tm + t) * pitch, n_sub)],
                                          sem.at[slot]).start(priority=k % 2)
            return carry
        lax.fori_loop(0, tm // 2, body, 0)

    def wait_all(slot):
        done = bufs[slot].at[pl.ds(0, top_k * tm * n_sub)]
        pltpu.make_async_copy(done, done, sem.at[slot]).wait()

    @pl.when(i == 0)
    def _():
        issue(0, 0)

    for slot in range(2):
        @pl.when(i % 2 == slot)
        def _():
            @pl.when(i + 1 < nb)
            def _():
                issue(i + 1, 1 - slot)
            wait_all(slot)
            for s in range(n_sub):
                cs_lo = slice(s * LANES, (s + 1) * LANES)
                cs_hi = slice((n_sub + s) * LANES, (n_sub + s + 1) * LANES)
                acc_lo = x_ref[:, cs_lo]
                acc_hi = x_ref[:, cs_hi]
                for k in range(top_k):
                    lo, hi = _unpack_bf16_pair(bufs[slot][pl.ds(k * tm * pitch + s, tm, stride=pitch), :])
                    acc_lo = acc_lo + lo
                    acc_hi = acc_hi + hi
                o_ref[:, cs_lo] = acc_lo
                o_ref[:, cs_hi] = acc_hi

    if final_norm:
        x = o_ref[...]
        ms = jnp.mean(x * x, axis=-1, keepdims=True)
        o_ref[...] = x * lax.rsqrt(ms + RMS_EPS) * g_ref[...]


def _combine(x2d, ys, pos, g_final, final_norm, cfg, tiles):
    n, d = x2d.shape
    n_sub = d // (2 * LANES)
    tm = min(tiles.combine_rows, n)
    top_k = cfg.top_k
    grid_spec = pltpu.PrefetchScalarGridSpec(
        num_scalar_prefetch=1,
        grid=(n // tm,),
        in_specs=[pl.BlockSpec((tm, d), lambda i, pos: (i, 0)),
                  pl.BlockSpec(memory_space=pl.ANY),
                  pl.BlockSpec((1, d), lambda i, pos: (0, 0))],
        out_specs=pl.BlockSpec((tm, d), lambda i, pos: (i, 0)),
        scratch_shapes=[pltpu.VMEM((top_k * tm * (n_sub + SLAB_PAD), LANES), jnp.uint32),
                        pltpu.VMEM((top_k * tm * (n_sub + SLAB_PAD), LANES), jnp.uint32),
                        pltpu.SemaphoreType.DMA((2,))],
    )
    return pl.pallas_call(
        functools.partial(_combine_kernel, tm=tm, n_sub=n_sub, top_k=top_k, final_norm=final_norm),
        grid_spec=grid_spec,
        out_shape=jax.ShapeDtypeStruct((n, d), f32),
        compiler_params=_params("arbitrary"),
        name="moe_combine",
    )(pos, x2d, ys, g_final.reshape(1, d).astype(f32))


def _layer(x, mem, p, l, is_last, g_final, cfg, tiles):
    b, s, d = x.shape
    n = b * s
    m_len = mem.shape[1]
    diff_width = d // 2
    conv_ch = d - diff_width
    x2d = x.reshape(n, d)

    h = _rmsnorm(x2d, p['g_mix'][l], tiles.norm_rows)
    head_dim = p['lambda_q1'].shape[-1]
    in_cols = p['w_in'].shape[-1]
    q_scale = jnp.where(jnp.arange(in_cols) < diff_width, head_dim ** -0.5 * LOG2E, 1.0)
    proj = _matmul([h], p['w_in'][l], None, bf16, tiles.mm_rows, tiles.mm_cols, "in_proj",
                   col_scale=q_scale)
    proj3 = proj.reshape(b, s, -1)
    lam_init = 0.8 - 0.6 * math.exp(-0.3 * l)
    a_out = _diff_attention(proj3, p['lambda_q1'][l], p['lambda_k1'][l], p['lambda_q2'][l],
                            p['lambda_k2'][l], p['g_subln'][l], lam_init, cfg, tiles)
    c_out = _conformer(proj3, 3 * diff_width, p['w_dw'][l], p['b_dw'][l], p['g_conv_ln'][l],
                       p['b_conv_ln'][l], p['w_conv_pw'][l].astype(bf16), cfg, tiles)
    assert diff_width == conv_ch
    x2d = _matmul([a_out.reshape(n, diff_width), c_out.reshape(n, conv_ch)],
                  p['w_out'][l], x2d, f32, tiles.mm_rows, tiles.mm_cols, "out_proj")

    hc = _rmsnorm(x2d, p['g_cross'][l], tiles.norm_rows)
    mem_n = _rmsnorm(mem.reshape(b * m_len, d), p['g_mem'][l], tiles.norm_rows)
    qc = _matmul([hc], p['w_cq'][l], None, bf16, tiles.mm_rows, tiles.mm_cols, "xq_proj")
    kc = _matmul([mem_n], p['w_ck'][l], None, bf16, tiles.mm_rows, tiles.mm_cols, "xk_proj")
    vc = _matmul([mem_n], p['w_cv'][l], None, bf16, tiles.mm_rows, tiles.mm_cols, "xv_proj")
    oc = _cross_attention(qc.reshape(b, s, d), kc.reshape(b, m_len, d), vc.reshape(b, m_len, d),
                          cfg, tiles)
    x2d = _matmul([oc.reshape(n, d)], p['w_co'][l], x2d, f32, tiles.mm_rows, tiles.mm_cols, "xo_proj")

    hg, top_e, gate_w, counts = _norm_router(x2d, p['g_ffn'][l], p['w_router'][l], p['b_router'][l],
                                             cfg, tiles)
    row_tok, row_w, runs, pos = _routing_plan(top_e, gate_w, counts, tiles.expert_rows)
    ys = _experts(hg, row_tok, row_w, runs, p['w_gate'][l], p['b_gate'][l], p['w_up'][l], p['b_up'][l],
                  p['w_down'][l], p['b_down'][l], tiles)
    x2d = _combine(x2d, ys, pos, g_final, is_last, cfg, tiles)
    return x2d.reshape(b, s, d)


def _forward(x, mem, params, g_final, cfg, tiles):
    depth = params['g_mix'].shape[0]
    for l in range(depth):
        x = _layer(x, mem, params, l, l == depth - 1, g_final, cfg, tiles)
    return x


def kernel(x, mem, g_mix, w_in, lambda_q1, lambda_k1, lambda_q2, lambda_k2, g_subln, w_dw, b_dw,
           g_conv_ln, b_conv_ln, w_conv_pw, w_out, g_cross, g_mem, w_cq, w_ck, w_cv, w_co, g_ffn,
           w_router, b_router, w_gate, b_gate, w_up, b_up, w_down, b_down, g_final):
    params = dict(g_mix=g_mix, w_in=w_in, lambda_q1=lambda_q1, lambda_k1=lambda_k1,
                  lambda_q2=lambda_q2, lambda_k2=lambda_k2, g_subln=g_subln, w_dw=w_dw, b_dw=b_dw,
                  g_conv_ln=g_conv_ln, b_conv_ln=b_conv_ln, w_conv_pw=w_conv_pw, w_out=w_out,
                  g_cross=g_cross, g_mem=g_mem, w_cq=w_cq, w_ck=w_ck, w_cv=w_cv, w_co=w_co,
                  g_ffn=g_ffn, w_router=w_router, b_router=b_router, w_gate=w_gate, b_gate=b_gate,
                  w_up=w_up, b_up=b_up, w_down=w_down, b_down=b_down)
    return _forward(x, mem, params, g_final, Cfg(), Tiles())
```
